```python
import jax, jax.numpy as jnp
from jax import lax
import numpy as np

D_MODEL = 2048
BATCH = 4
SEQ = 2048
DEPTH = 1
DEC_BATCH = 32
DEC_SEQ = 1
PAST_LEN = 8192
PAGE_SIZE = 128

SB_HEADS = 8
SB_HEAD_DIM = 128
SB_WIDTH = SB_HEADS * SB_HEAD_DIM
SB_BIAS_INIT = -8.0
Q_BLOCK = 128
GLA_HEADS = 4
GLA_DK = 128
GLA_DV = 256
GLA_KW = GLA_HEADS * GLA_DK
GLA_VW = GLA_HEADS * GLA_DV
GLA_GATE_RANK = 16
GLA_GATE_NORM = 16.0
GLA_CHUNK = 64
MIX_WIDTH = SB_WIDTH + GLA_VW
IN_SPLIT_POINTS = (SB_WIDTH, 2 * SB_WIDTH, 3 * SB_WIDTH,
                   3 * SB_WIDTH + GLA_KW, 3 * SB_WIDTH + 2 * GLA_KW,
                   3 * SB_WIDTH + 2 * GLA_KW + GLA_VW,
                   3 * SB_WIDTH + 2 * GLA_KW + 2 * GLA_VW)
IN_WIDTH = 3 * SB_WIDTH + 2 * GLA_KW + 2 * GLA_VW + GLA_GATE_RANK
D_FF = -(-(8 * D_MODEL) // (3 * 256)) * 256
N_MOD = 6
EPS = 1e-6

kernel_name = "stickbreak_gla_hymba_step"


def rmsnorm(x, g):
    xf = x.astype(jnp.float32)
    y = xf * lax.rsqrt(jnp.mean(xf * xf, axis=-1, keepdims=True) + EPS)
    return (y * g.astype(jnp.float32)).astype(x.dtype)


def sb_block(q, k, v, bias, q_pos, k_pos):
    z = jnp.einsum('bhqd,bhkd->bhqk', q.astype(jnp.float32), k.astype(jnp.float32)) * (SB_HEAD_DIM ** -0.5)
    z = z + bias.astype(jnp.float32)[None, :, None, None]
    mask = k_pos[None, :] < q_pos[:, None]
    log_beta = jax.nn.log_sigmoid(z)
    log_keep = jnp.where(mask, jax.nn.log_sigmoid(-z), 0.0)
    tail = lax.cumsum(log_keep, axis=3, reverse=True) - log_keep
    a = jnp.where(mask, jnp.exp(log_beta + tail), 0.0)
    return jnp.einsum('bhqk,bhkd->bhqd', a, v.astype(jnp.float32))


def sb_prompt(q, k, v, bias):
    B, H, S, dh = q.shape
    nb = S // Q_BLOCK
    qb = q.reshape(B, H, nb, Q_BLOCK, dh).transpose(2, 0, 1, 3, 4)
    pos = jnp.arange(S)
    qpos = pos.reshape(nb, Q_BLOCK)
    out = lax.map(lambda a: sb_block(a[0], k, v, bias, a[1], pos), (qb, qpos))
    return out.transpose(1, 2, 0, 3, 4).reshape(B, H, S, dh)


def gla_chunked(q, k, v, g, s0, chunk):
    B, H, T, dk = q.shape
    dv = v.shape[-1]
    n = T // chunk

    def to_chunks(a):
        return a.astype(jnp.float32).reshape(B, H, n, chunk, a.shape[-1]).transpose(2, 0, 1, 3, 4)

    qc, kc, vc, gc = to_chunks(q), to_chunks(k), to_chunks(v), to_chunks(g)
    causal = jnp.tril(jnp.ones((chunk, chunk), bool))[:, :, None]

    def step(S, inp):
        qi, ki, vi, gi = inp
        b = jnp.cumsum(gi, axis=2)
        o_inter = jnp.einsum('bhtd,bhdv->bhtv', qi * jnp.exp(b), S)
        diff = b[:, :, :, None, :] - b[:, :, None, :, :]
        decay = jnp.exp(jnp.where(causal, diff, -jnp.inf))
        att = jnp.einsum('bhtd,bhsd,bhtsd->bhts', qi, ki, decay)
        o = o_inter + jnp.einsum('bhts,bhsv->bhtv', att, vi)
        b_last = b[:, :, -1:, :]
        S_new = jnp.exp(b_last[:, :, 0, :])[..., None] * S + \
            jnp.einsum('bhsd,bhsv->bhdv', ki * jnp.exp(b_last - b), vi)
        return S_new, o

    S_fin, o = lax.scan(step, s0.astype(jnp.float32), (qc, kc, vc, gc))
    o = o.transpose(1, 2, 0, 3, 4).reshape(B, H, T, dv)
    return o, S_fin


def token_mixer(h, w_in, b_sb, w_gate_up, b_gate, g_sb_out, g_gla_out, w_out, k_past, v_past, s0):
    B, T, _ = h.shape
    proj = h @ w_in
    q_sb, k_sb, v_sb, q_g, k_g, v_g, r_g, g_low = jnp.split(proj, IN_SPLIT_POINTS, axis=-1)

    def to_heads(a, nh):
        return a.reshape(B, T, nh, -1).transpose(0, 2, 1, 3)

    k_rows = k_sb.reshape(B, T, SB_HEADS, SB_HEAD_DIM)
    v_rows = v_sb.reshape(B, T, SB_HEADS, SB_HEAD_DIM)
    q_h = to_heads(q_sb, SB_HEADS)
    if k_past is None:
        o_sb = sb_prompt(q_h, to_heads(k_sb, SB_HEADS), to_heads(v_sb, SB_HEADS), b_sb)
        s0 = jnp.zeros((B, GLA_HEADS, GLA_DK, GLA_DV), jnp.float32)
        gla_chunk = min(GLA_CHUNK, T)
    else:
        past_len = k_past.shape[1]
        k_all = jnp.concatenate([k_past.astype(h.dtype), k_rows], axis=1).transpose(0, 2, 1, 3)
        v_all = jnp.concatenate([v_past.astype(h.dtype), v_rows], axis=1).transpose(0, 2, 1, 3)
        o_sb = sb_block(q_h, k_all, v_all, b_sb, past_len + jnp.arange(T), jnp.arange(past_len + T))
        gla_chunk = T
    o_sb = rmsnorm(o_sb.astype(h.dtype).transpose(0, 2, 1, 3), g_sb_out).reshape(B, T, SB_WIDTH)

    log_alpha = jax.nn.log_sigmoid((g_low @ w_gate_up + b_gate).astype(jnp.float32)) / GLA_GATE_NORM
    o_g, s_new = gla_chunked(to_heads(q_g, GLA_HEADS) * (GLA_DK ** -0.5), to_heads(k_g, GLA_HEADS),
                             to_heads(v_g, GLA_HEADS), to_heads(log_alpha, GLA_HEADS), s0, gla_chunk)
    o_g = rmsnorm(o_g.astype(h.dtype).transpose(0, 2, 1, 3), g_gla_out).reshape(B, T, GLA_VW)
    o_g = o_g * jax.nn.silu(r_g)

    out = jnp.concatenate([o_sb, o_g], axis=-1) @ w_out
    return out, k_rows, v_rows, s_new.astype(h.dtype)


def decoder_layer(x, c, w_ada, b_ada, g_pre_mix, w_in, b_sb, w_gate_up, b_gate, g_sb_out, g_gla_out,
                  w_out, g_post_mix, g_pre_ffn, w_ffn_gate, w_ffn_up, w_ffn_down, g_post_ffn,
                  k_past, v_past, s0):
    mod = (jax.nn.silu(c) @ w_ada + b_ada)[:, None, :]
    sh1, sc1, gt1, sh2, sc2, gt2 = jnp.split(mod, N_MOD, axis=-1)
    h = rmsnorm(x, g_pre_mix) * (1.0 + sc1) + sh1
    m, k_rows, v_rows, s_new = token_mixer(h, w_in, b_sb, w_gate_up, b_gate, g_sb_out, g_gla_out, w_out,
                                           k_past, v_past, s0)
    x = x + gt1 * rmsnorm(m, g_post_mix)
    h = rmsnorm(x, g_pre_ffn) * (1.0 + sc2) + sh2
    f = (jax.nn.silu(h @ w_ffn_gate) * (h @ w_ffn_up)) @ w_ffn_down
    x = x + gt2 * rmsnorm(f, g_post_ffn)
    return x, k_rows, v_rows, s_new


def setup_inputs(seed: int = 0) -> dict:
    key = jax.random.key(seed)
    ks = jax.random.split(key, 24)
    f32 = jnp.float32
    n_pages = PAST_LEN // PAGE_SIZE
    n_used = DEC_BATCH * n_pages
    n_pool = n_used + n_used // 4

    def nrm(k, shape, scale):
        return jax.random.normal(k, shape, f32) * scale

    def gain(k, n):
        return 1.0 + 0.05 * jax.random.normal(k, (DEPTH, n), f32)

    page_table = jax.random.permutation(ks[6], n_pool)[:n_used].reshape(DEC_BATCH, n_pages).astype(jnp.int32)
    return {
        "x_prompt": nrm(ks[0], (BATCH, SEQ, D_MODEL), 1.0),
        "x_sample": nrm(ks[1], (DEC_BATCH, DEC_SEQ, D_MODEL), 1.0),
        "c_prompt": nrm(ks[2], (BATCH, D_MODEL), 1.0),
        "c_sample": nrm(ks[3], (DEC_BATCH, D_MODEL), 1.0),
        "cache_k": nrm(ks[4], (DEPTH, n_pool, PAGE_SIZE, SB_HEADS, SB_HEAD_DIM), 1.0),
        "cache_v": nrm(ks[5], (DEPTH, n_pool, PAGE_SIZE, SB_HEADS, SB_HEAD_DIM), 1.0),
        "page_table": page_table,
        "state_gla": nrm(ks[7], (DEPTH, DEC_BATCH, GLA_HEADS, GLA_DK, GLA_DV), 0.1),
        "w_ada": nrm(ks[8], (DEPTH, D_MODEL, N_MOD * D_MODEL), 0.5 * D_MODEL ** -0.5),
        "b_ada": nrm(ks[9], (DEPTH, N_MOD * D_MODEL), 0.02),
        "g_pre_mix": gain(ks[10], D_MODEL),
        "w_in": nrm(ks[11], (DEPTH, D_MODEL, IN_WIDTH), D_MODEL ** -0.5),
        "b_sb": SB_BIAS_INIT + 0.1 * jax.random.normal(ks[23], (DEPTH, SB_HEADS), f32),
        "w_gate_up": nrm(ks[12], (DEPTH, GLA_GATE_RANK, GLA_KW), GLA_GATE_RANK ** -0.5),
        "b_gate": nrm(ks[13], (DEPTH, GLA_KW), 0.1),
        "g_sb_out": gain(ks[14], SB_HEAD_DIM),
        "g_gla_out": gain(ks[15], GLA_DV),
        "w_out": nrm(ks[16], (DEPTH, MIX_WIDTH, D_MODEL), MIX_WIDTH ** -0.5),
        "g_post_mix": gain(ks[17], D_MODEL),
        "g_pre_ffn": gain(ks[18], D_MODEL),
        "w_ffn_gate": nrm(ks[19], (DEPTH, D_MODEL, D_FF), D_MODEL ** -0.5),
        "w_ffn_up": nrm(ks[20], (DEPTH, D_MODEL, D_FF), D_MODEL ** -0.5),
        "w_ffn_down": nrm(ks[21], (DEPTH, D_FF, D_MODEL), D_FF ** -0.5),
        "g_post_ffn": gain(ks[22], D_MODEL),
    }


def reference(x_prompt, x_sample, c_prompt, c_sample, cache_k, cache_v, page_table, state_gla,
              w_ada, b_ada, g_pre_mix, w_in, b_sb, w_gate_up, b_gate, g_sb_out, g_gla_out, w_out,
              g_post_mix, g_pre_ffn, w_ffn_gate, w_ffn_up, w_ffn_down, g_post_ffn):
    yp, ys = x_prompt, x_sample
    dec_b = page_table.shape[0]
    kp_l, vp_l, sp_l, ks_l, vs_l, ss_l = [], [], [], [], [], []
    for l in range(DEPTH):
        w = (w_ada[l], b_ada[l], g_pre_mix[l], w_in[l], b_sb[l], w_gate_up[l], b_gate[l], g_sb_out[l],
             g_gla_out[l], w_out[l], g_post_mix[l], g_pre_ffn[l], w_ffn_gate[l], w_ffn_up[l],
             w_ffn_down[l], g_post_ffn[l])
        yp, kp, vp, sp = decoder_layer(yp, c_prompt, *w, None, None, None)
        k_past = cache_k[l][page_table].reshape(dec_b, -1, SB_HEADS, SB_HEAD_DIM)
        v_past = cache_v[l][page_table].reshape(dec_b, -1, SB_HEADS, SB_HEAD_DIM)
        ys, ks_, vs_, ss = decoder_layer(ys, c_sample, *w, k_past, v_past, state_gla[l])
        kp_l.append(kp); vp_l.append(vp); sp_l.append(sp)
        ks_l.append(ks_); vs_l.append(vs_); ss_l.append(ss)
    return (yp, ys, jnp.stack(kp_l), jnp.stack(vp_l), jnp.stack(sp_l),
            jnp.stack(ks_l), jnp.stack(vs_l), jnp.stack(ss_l))
```

```python
import functools

import jax
import jax.numpy as jnp
from jax import lax
from jax.experimental import pallas as pl
from jax.experimental.pallas import tpu as pltpu

F32 = jnp.float32
BF16 = jnp.bfloat16

LANES = 128
SUBLANES = 8
VMEM_LIMIT_BYTES = 52 * 1024 * 1024

EPS = 1e-6
GLA_GATE_NORM = 16.0
GLA_CHUNK = 64
GLA_SUB = 16
KEY_BLOCK = 128


def _params(*sem):
    return pltpu.CompilerParams(dimension_semantics=sem, vmem_limit_bytes=VMEM_LIMIT_BYTES)


def _log_sigmoid(z):
    return jnp.minimum(z, 0.0) - jnp.log1p(jnp.exp(-jnp.abs(z)))


def _silu(x):
    return x * (1.0 / (1.0 + jnp.exp(-x)))


def _rms(x, g):
    ms = jnp.mean(x * x, axis=-1, keepdims=True)
    return x * lax.rsqrt(ms + EPS) * g


def _split_bf16(x, pieces):
    out = []
    for _ in range(pieces - 1):
        p = x.astype(BF16)
        out.append(p)
        x = x - p.astype(F32)
    out.append(x.astype(BF16))
    return out


def _dot(a, b):
    return jnp.dot(a, b, preferred_element_type=F32)


def _dot_nt(a, b):
    return lax.dot_general(a, b, (((1,), (1,)), ((), ())), preferred_element_type=F32)


def _dot_tn(a, b):
    return lax.dot_general(a, b, (((0,), (0,)), ((), ())), preferred_element_type=F32)


def _ada_kernel(c_ref, w_ref, b_ref, o_ref):
    a = _silu(c_ref[...]).astype(BF16)
    o_ref[...] = _dot(a, w_ref[...].astype(BF16)) + b_ref[...]


def _ada(c, w, b, tn=1024):
    m, d = c.shape
    n = w.shape[1]
    return pl.pallas_call(
        _ada_kernel,
        grid=(n // tn,),
        in_specs=[pl.BlockSpec((m, d), lambda j: (0, 0)),
                  pl.BlockSpec((d, tn), lambda j: (0, j)),
                  pl.BlockSpec((1, tn), lambda j: (0, j))],
        out_specs=pl.BlockSpec((m, tn), lambda j: (0, j)),
        out_shape=jax.ShapeDtypeStruct((m, n), F32),
        compiler_params=_params("arbitrary"),
        name="ada_mod",
    )(c, w, b)


def _in_proj_kernel(x_ref, g_ref, sc_ref, sh_ref, w_ref, wl_ref, o_ref, ol_ref, h_scr):
    @pl.when(pl.program_id(1) == 0)
    def _():
        h = _rms(x_ref[...], g_ref[...]) * (1.0 + sc_ref[...]) + sh_ref[...]
        h_scr[...] = h.astype(BF16)
        ol_ref[...] = _dot(h_scr[...], wl_ref[...])

    o_ref[...] = _dot(h_scr[...], w_ref[...])


def _in_proj(x, g, sc, sh, w, wl, tm, rows_per_group, tn=1024):
    m, d = x.shape
    n = w.shape[1]
    r = sc.shape[1]
    mod_spec = pl.BlockSpec((None, r, d), lambda i, j: ((i * tm) // rows_per_group, 0, 0))
    return pl.pallas_call(
        _in_proj_kernel,
        grid=(m // tm, n // tn),
        in_specs=[pl.BlockSpec((tm, d), lambda i, j: (i, 0)),
                  pl.BlockSpec((1, d), lambda i, j: (0, 0)),
                  mod_spec, mod_spec,
                  pl.BlockSpec((d, tn), lambda i, j: (0, j)),
                  pl.BlockSpec((d, LANES), lambda i, j: (0, 0))],
        out_specs=[pl.BlockSpec((tm, tn), lambda i, j: (i, j)),
                   pl.BlockSpec((tm, LANES), lambda i, j: (i, 0))],
        out_shape=[jax.ShapeDtypeStruct((m, n), F32),
                   jax.ShapeDtypeStruct((m, LANES), F32)],
        scratch_shapes=[pltpu.VMEM((tm, d), BF16)],
        compiler_params=_params("arbitrary", "arbitrary"),
        name="in_proj",
    )(x, g, sc, sh, w, wl)


def _sb_block(z, carry, acc, v_bf16, uo, mask):
    lb = _log_sigmoid(z)
    lk = lb - z
    if mask is not None:
        lk = jnp.where(mask, lk, 0.0)
    hi, lo = _split_bf16(lk, 2)
    ts = _dot(hi, uo) + _dot(lo, uo)
    a = jnp.exp(lb + ts[:, :KEY_BLOCK] + carry)
    if mask is not None:
        a = jnp.where(mask, a, 0.0)
    acc = acc + _dot(a.astype(BF16), v_bf16)
    return carry + ts[:, KEY_BLOCK:], acc


def _sb_prompt_kernel(q_ref, k_ref, v_ref, bias_ref, uo_ref, o_ref, *, tq, scale):
    qi = pl.program_id(2)
    h = pl.program_id(1)
    q = (q_ref[...] * scale).astype(BF16)
    bias = bias_ref[h]
    uo = uo_ref[...]
    n_diag = tq // KEY_BLOCK
    row = lax.broadcasted_iota(jnp.int32, (tq, KEY_BLOCK), 0)
    col = lax.broadcasted_iota(jnp.int32, (tq, KEY_BLOCK), 1)

    def block(j, carry, acc, mask):
        start = pl.multiple_of(j * KEY_BLOCK, KEY_BLOCK)
        kb = k_ref[pl.ds(start, KEY_BLOCK), :].astype(BF16)
        vb = v_ref[pl.ds(start, KEY_BLOCK), :].astype(BF16)
        z = _dot_nt(q, kb) + bias
        return _sb_block(z, carry, acc, vb, uo, mask)

    carry = jnp.zeros((tq, KEY_BLOCK), F32)
    acc = jnp.zeros((tq, KEY_BLOCK), F32)
    for d in reversed(range(n_diag)):
        mask = (col + d * KEY_BLOCK) < row
        carry, acc = block(qi * n_diag + d, carry, acc, mask)

    def body(jj, ca):
        j = qi * n_diag - 1 - jj
        return block(j, ca[0], ca[1], None)

    carry, acc = lax.fori_loop(0, qi * n_diag, body, (carry, acc))
    o_ref[...] = acc


def _sb_prompt(proj, b_sb, uo, batch, seq, heads, dh, tq):
    m = proj.shape[0]
    nq = seq // tq
    kernel = functools.partial(_sb_prompt_kernel, tq=tq, scale=dh ** -0.5)
    return pl.pallas_call(
        kernel,
        grid=(batch, heads, nq),
        in_specs=[pl.BlockSpec((tq, dh), lambda b, h, i: (b * nq + i, h)),
                  pl.BlockSpec((seq, dh), lambda b, h, i: (b, heads + h)),
                  pl.BlockSpec((seq, dh), lambda b, h, i: (b, 2 * heads + h)),
                  pl.BlockSpec(memory_space=pltpu.SMEM),
                  pl.BlockSpec((KEY_BLOCK, 2 * KEY_BLOCK), lambda b, h, i: (0, 0))],
        out_specs=pl.BlockSpec((tq, dh), lambda b, h, i: (b * nq + i, h)),
        out_shape=jax.ShapeDtypeStruct((m, heads * dh), F32),
        compiler_params=_params("arbitrary", "arbitrary", "arbitrary"),
        name="sb_prompt",
    )(proj, proj, proj, b_sb, uo)


def _sb_decode_kernel(pt_ref, q_ref, bias_ref, uo_ref, *refs, pages, heads, scale):
    k_refs, v_refs = refs[:pages], refs[pages:2 * pages]
    o_ref, carry_scr, acc_scr = refs[2 * pages:]
    g = pl.program_id(1)

    @pl.when(g == 0)
    def _():
        carry_scr[...] = jnp.zeros_like(carry_scr)
        acc_scr[...] = jnp.zeros_like(acc_scr)

    q = (q_ref[...] * scale).astype(BF16)
    bias = bias_ref[...]
    uo = uo_ref[...]
    head_of_row = lax.broadcasted_iota(jnp.int32, (heads, KEY_BLOCK), 0)
    carry = carry_scr[...]
    acc = acc_scr[...]
    for p in range(pages):
        z = jnp.zeros((heads, KEY_BLOCK), F32)
        for h in range(heads):
            kh = k_refs[p][pl.ds(h, KEY_BLOCK, stride=heads), :].astype(BF16)
            z = jnp.where(head_of_row == h, _dot_nt(q, kh), z)
        z = z + bias
        lb = _log_sigmoid(z)
        lk = lb - z
        hi, lo = _split_bf16(lk, 2)
        ts = _dot(hi, uo) + _dot(lo, uo)
        a = jnp.exp(lb + ts[:, :KEY_BLOCK] + carry).astype(BF16)
        carry = carry + ts[:, KEY_BLOCK:]
        for h in range(heads):
            vh = v_refs[p][pl.ds(h, KEY_BLOCK, stride=heads), :].astype(BF16)
            acc = acc + jnp.where(head_of_row == h, _dot(a, vh), 0.0)
    carry_scr[...] = carry
    acc_scr[...] = acc

    @pl.when(g == pl.num_programs(1) - 1)
    def _():
        o_ref[...] = acc


def _sb_decode(q, cache_k, cache_v, page_table, bias_rows, uo, pages=8):
    nb, heads, dh = q.shape
    n_pages = page_table.shape[1]
    rows = cache_k.shape[1]
    groups = n_pages // pages
    kernel = functools.partial(_sb_decode_kernel, pages=pages, heads=heads, scale=dh ** -0.5)

    def page_spec(i):
        return pl.BlockSpec((None, rows, dh),
                            lambda b, g, pt, i=i: (pt[b, n_pages - 1 - (g * pages + i)], 0, 0))

    grid_spec = pltpu.PrefetchScalarGridSpec(
        num_scalar_prefetch=1,
        grid=(nb, groups),
        in_specs=[pl.BlockSpec((None, heads, dh), lambda b, g, pt: (b, 0, 0)),
                  pl.BlockSpec((heads, KEY_BLOCK), lambda b, g, pt: (0, 0)),
                  pl.BlockSpec((KEY_BLOCK, 2 * KEY_BLOCK), lambda b, g, pt: (0, 0))]
                 + [page_spec(i) for i in range(pages)] * 2,
        out_specs=pl.BlockSpec((None, heads, dh), lambda b, g, pt: (b, 0, 0)),
        scratch_shapes=[pltpu.VMEM((heads, KEY_BLOCK), F32), pltpu.VMEM((heads, dh), F32)],
    )
    return pl.pallas_call(
        kernel,
        grid_spec=grid_spec,
        out_shape=jax.ShapeDtypeStruct((nb, heads, dh), F32),
        compiler_params=_params("arbitrary", "arbitrary"),
        name="sb_decode",
    )(page_table, q, bias_rows, uo, *([cache_k] * pages), *([cache_v] * pages))


def _gla_prefix_matrix():
    t = lax.broadcasted_iota(jnp.int32, (2 * GLA_CHUNK, GLA_CHUNK), 0)
    s = lax.broadcasted_iota(jnp.int32, (2 * GLA_CHUNK, GLA_CHUNK), 1)
    bound = jnp.where(t < GLA_CHUNK, t + 1, ((t - GLA_CHUNK) // GLA_SUB) * GLA_SUB)
    return jnp.where(s < bound, 1.0, 0.0).astype(BF16)


def _gla_chunk(q, k, v, g, st, ones_bf16, ll):
    c, dk = q.shape
    n_sub = c // GLA_SUB
    bb = sum(_dot(ll, piece) for piece in _split_bf16(g, 3))
    b, r = bb[:c], bb[c:]
    qt = (q * jnp.exp(b - r)).astype(BF16)
    row = lax.broadcasted_iota(jnp.int32, (c, dk), 0)
    lane = lax.broadcasted_iota(jnp.int32, (GLA_SUB, LANES), 1)
    trow = lax.broadcasted_iota(jnp.int32, (GLA_SUB, LANES), 0)
    strips = []
    for i in range(n_sub):
        lo_, hi_ = i * GLA_SUB, (i + 1) * GLA_SUB
        qi, ki, bi = q[lo_:hi_], k[lo_:hi_], b[lo_:hi_]
        prods = [qi * ki[s:s + 1] * jnp.exp(jnp.minimum(bi - bi[s:s + 1], 0.0))
                 for s in range(GLA_SUB)]
        rsum = _dot(jnp.concatenate(prods, axis=0).astype(BF16), ones_bf16)
        strip = jnp.zeros((GLA_SUB, LANES), F32)
        for s in range(GLA_SUB):
            strip = jnp.where(lane == lo_ + s, rsum[s * GLA_SUB:(s + 1) * GLA_SUB], strip)
        strip = jnp.where(lane - lo_ <= trow, strip, 0.0)[:, :c]
        if i > 0:
            kd = jnp.where(row < lo_, k * jnp.exp(jnp.minimum(r[lo_:lo_ + 1] - b, 0.0)), 0.0)
            strip = strip + _dot_nt(qt[lo_:hi_], kd.astype(BF16))
        strips.append(strip)
    att = jnp.concatenate(strips, axis=0).astype(BF16)
    v16 = v.astype(BF16)
    o = _dot_nt((q * jnp.exp(b)).astype(BF16), st.astype(BF16)) + _dot(att, v16)
    b_last = b[c - 1:c]
    kdec = (k * jnp.exp(b_last - b)).astype(BF16)
    st_new = st * jnp.exp(b_last) + _dot_tn(v16, kdec)
    return o, st_new


def _gla_prompt_kernel(q_ref, k_ref, v_ref, gl_ref, wg_ref, bg_ref, ones_ref, ll_ref, o_ref,
                       st_ref, st_scr, *, n_inner, scale):
    @pl.when(pl.program_id(2) == 0)
    def _():
        st_scr[...] = jnp.zeros_like(st_scr)

    wg = wg_ref[...]
    bg = bg_ref[...]
    ones = ones_ref[...]
    ll = ll_ref[...]
    st = st_scr[...]
    for ci in range(n_inner):
        sl = pl.ds(ci * GLA_CHUNK, GLA_CHUNK)
        g = _log_sigmoid(_dot(gl_ref[sl, :].astype(BF16), wg) + bg) * (1.0 / GLA_GATE_NORM)
        o, st = _gla_chunk(q_ref[sl, :] * scale, k_ref[sl, :], v_ref[sl, :], g, st, ones, ll)
        o_ref[sl, :] = o
    st_scr[...] = st
    st_ref[...] = st


def _gla_prompt(proj, glow, wgu, b_gate, ones, batch, seq, heads, dk, dv, q_col, k_col, v_col,
                n_inner=4):
    m = proj.shape[0]
    cb = n_inner * GLA_CHUNK
    nc = seq // cb
    kernel = functools.partial(_gla_prompt_kernel, n_inner=n_inner, scale=dk ** -0.5)
    return pl.pallas_call(
        kernel,
        grid=(batch, heads, nc),
        in_specs=[pl.BlockSpec((cb, dk), lambda b, h, c: (b * nc + c, q_col // dk + h)),
                  pl.BlockSpec((cb, dk), lambda b, h, c: (b * nc + c, k_col // dk + h)),
                  pl.BlockSpec((cb, dv), lambda b, h, c: (b * nc + c, v_col // dv + h)),
                  pl.BlockSpec((cb, LANES), lambda b, h, c: (b * nc + c, 0)),
                  pl.BlockSpec((LANES, dk), lambda b, h, c: (0, h)),
                  pl.BlockSpec((1, dk), lambda b, h, c: (0, h)),
                  pl.BlockSpec((LANES, LANES), lambda b, h, c: (0, 0)),
                  pl.BlockSpec((2 * GLA_CHUNK, GLA_CHUNK), lambda b, h, c: (0, 0))],
        out_specs=[pl.BlockSpec((cb, dv), lambda b, h, c: (b * nc + c, h)),
                   pl.BlockSpec((None, None, dv, dk), lambda b, h, c: (b, h, 0, 0))],
        out_shape=[jax.ShapeDtypeStruct((m, heads * dv), F32),
                   jax.ShapeDtypeStruct((batch, heads, dv, dk), F32)],
        scratch_shapes=[pltpu.VMEM((dv, dk), F32)],
        compiler_params=_params("arbitrary", "arbitrary", "arbitrary"),
        name="gla_prompt",
    )(proj, proj, proj, glow, wgu, b_gate, ones, _gla_prefix_matrix())


def _gla_step_kernel(q_ref, k_ref, v_ref, gl_ref, wg_ref, bg_ref, s_ref, o_ref, so_ref, *,
                     heads, dk, dv, scale):
    g = _log_sigmoid(_dot(gl_ref[...].astype(BF16), wg_ref[...]) + bg_ref[...]) * (1.0 / GLA_GATE_NORM)
    q = q_ref[...] * scale
    k = k_ref[...]
    v = v_ref[...]
    eye = (lax.broadcasted_iota(jnp.int32, (dk, dk), 0)
           == lax.broadcasted_iota(jnp.int32, (dk, dk), 1))

    def column(x_row):
        return jnp.sum(jnp.where(eye, x_row, 0.0), axis=1, keepdims=True)

    for h in range(heads):
        ksl = slice(h * dk, (h + 1) * dk)
        vsl = slice(h * dv, (h + 1) * dv)
        s_new = column(jnp.exp(g[:, ksl])) * s_ref[h] + column(k[:, ksl]) * v[:, vsl]
        so_ref[h] = s_new
        o_ref[:, vsl] = jnp.sum(column(q[:, ksl]) * s_new, axis=0, keepdims=True)


def _gla_step(proj, glow, wgu, b_gate, state, dk, dv, q_col, k_col, v_col):
    nb, heads = state.shape[0], state.shape[1]
    kw, vw = heads * dk, heads * dv
    kernel = functools.partial(_gla_step_kernel, heads=heads, dk=dk, dv=dv, scale=dk ** -0.5)
    proj3 = proj.reshape(nb, 1, proj.shape[1])
    glow3 = glow.reshape(nb, 1, LANES)
    o, s_new = pl.pallas_call(
        kernel,
        grid=(nb,),
        in_specs=[pl.BlockSpec((None, 1, kw), lambda b: (b, 0, q_col // kw)),
                  pl.BlockSpec((None, 1, kw), lambda b: (b, 0, k_col // kw)),
                  pl.BlockSpec((None, 1, vw), lambda b: (b, 0, v_col // vw)),
                  pl.BlockSpec((None, 1, LANES), lambda b: (b, 0, 0)),
                  pl.BlockSpec((LANES, kw), lambda b: (0, 0)),
                  pl.BlockSpec((1, kw), lambda b: (0, 0)),
                  pl.BlockSpec((None, heads, dk, dv), lambda b: (b, 0, 0, 0))],
        out_specs=[pl.BlockSpec((None, 1, vw), lambda b: (b, 0, 0)),
                   pl.BlockSpec((None, heads, dk, dv), lambda b: (b, 0, 0, 0))],
        out_shape=[jax.ShapeDtypeStruct((nb, 1, vw), F32),
                   jax.ShapeDtypeStruct(state.shape, F32)],
        compiler_params=_params("arbitrary"),
        name="gla_step",
    )(proj3, proj3, proj3, glow3, wgu, b_gate, state)
    return o.reshape(nb, vw), s_new


def _out_proj_kernel(osb_ref, og_ref, rg_ref, x_ref, gsb_ref, ggla_ref, w_ref, gpost_ref,
                     gt_ref, gpre_ref, sc_ref, sh_ref, x1_ref, h2_ref, mix_scr, *, dh, dv):
    sbw = osb_ref.shape[1]
    for h in range(sbw // dh):
        sl = slice(h * dh, (h + 1) * dh)
        mix_scr[:, sl] = _rms(osb_ref[:, sl], gsb_ref[...]).astype(BF16)
    for h in range(og_ref.shape[1] // dv):
        sl = slice(h * dv, (h + 1) * dv)
        y = _rms(og_ref[:, sl], ggla_ref[...]) * _silu(rg_ref[:, sl])
        mix_scr[:, sbw + h * dv:sbw + (h + 1) * dv] = y.astype(BF16)
    m = _dot(mix_scr[...], w_ref[...])
    x1 = x_ref[...] + gt_ref[...] * _rms(m, gpost_ref[...])
    x1_ref[...] = x1
    h2 = _rms(x1, gpre_ref[...]) * (1.0 + sc_ref[...]) + sh_ref[...]
    h2_ref[...] = h2.astype(BF16)


def _out_proj(osb, og, proj, rg_col, x, g_sb, g_gla, w, g_post, gt, g_pre, sc, sh, tm,
              rows_per_group):
    m, d = x.shape
    sbw, gw = osb.shape[1], og.shape[1]
    dh, dv = g_sb.shape[1], g_gla.shape[1]
    r = gt.shape[1]
    mod_spec = pl.BlockSpec((None, r, d), lambda i: ((i * tm) // rows_per_group, 0, 0))
    row = lambda width: pl.BlockSpec((1, width), lambda i: (0, 0))
    kernel = functools.partial(_out_proj_kernel, dh=dh, dv=dv)
    return pl.pallas_call(
        kernel,
        grid=(m // tm,),
        in_specs=[pl.BlockSpec((tm, sbw), lambda i: (i, 0)),
                  pl.BlockSpec((tm, gw), lambda i: (i, 0)),
                  pl.BlockSpec((tm, gw), lambda i: (i, rg_col // gw)),
                  pl.BlockSpec((tm, d), lambda i: (i, 0)),
                  row(dh), row(dv),
                  pl.BlockSpec((sbw + gw, d), lambda i: (0, 0)),
                  row(d), mod_spec, row(d), mod_spec, mod_spec],
        out_specs=[pl.BlockSpec((tm, d), lambda i: (i, 0)),
                   pl.BlockSpec((tm, d), lambda i: (i, 0))],
        out_shape=[jax.ShapeDtypeStruct((m, d), F32),
                   jax.ShapeDtypeStruct((m, d), BF16)],
        scratch_shapes=[pltpu.VMEM((tm, sbw + gw), BF16)],
        compiler_params=_params("arbitrary"),
        name="out_proj",
    )(osb, og, proj, x, g_sb, g_gla, w, g_post, gt, g_pre, sc, sh)


def _ffn_kernel(h_ref, wg_ref, wu_ref, wd_ref, x1_ref, gpost_ref, gt_ref, y_ref, acc_scr):
    j = pl.program_id(1)

    @pl.when(j == 0)
    def _():
        acc_scr[...] = jnp.zeros_like(acc_scr)

    h = h_ref[...]
    t = _silu(_dot(h, wg_ref[...])) * _dot(h, wu_ref[...])
    acc_scr[...] += _dot(t.astype(BF16), wd_ref[...])

    @pl.when(j == pl.num_programs(1) - 1)
    def _():
        y_ref[...] = x1_ref[...] + gt_ref[...] * _rms(acc_scr[...], gpost_ref[...])


def _ffn(h2, wg, wu, wd, x1, g_post, gt, tm, rows_per_group, tf=512):
    m, d = x1.shape
    f = wg.shape[1]
    r = gt.shape[1]
    return pl.pallas_call(
        _ffn_kernel,
        grid=(m // tm, f // tf),
        in_specs=[pl.BlockSpec((tm, d), lambda i, j: (i, 0)),
                  pl.BlockSpec((d, tf), lambda i, j: (0, j)),
                  pl.BlockSpec((d, tf), lambda i, j: (0, j)),
                  pl.BlockSpec((tf, d), lambda i, j: (j, 0)),
                  pl.BlockSpec((tm, d), lambda i, j: (i, 0)),
                  pl.BlockSpec((1, d), lambda i, j: (0, 0)),
                  pl.BlockSpec((None, r, d), lambda i, j: ((i * tm) // rows_per_group, 0, 0))],
        out_specs=pl.BlockSpec((tm, d), lambda i, j: (i, 0)),
        out_shape=jax.ShapeDtypeStruct((m, d), F32),
        scratch_shapes=[pltpu.VMEM((tm, d), F32)],
        compiler_params=_params("arbitrary", "arbitrary"),
        name="ffn",
    )(h2, wg, wu, wd, x1, g_post, gt)


def _suffix_sum_matrix():
    j = lax.broadcasted_iota(jnp.int32, (KEY_BLOCK, 2 * KEY_BLOCK), 0)
    s = lax.broadcasted_iota(jnp.int32, (KEY_BLOCK, 2 * KEY_BLOCK), 1)
    return jnp.where((j > s) | (s >= KEY_BLOCK), 1.0, 0.0).astype(BF16)


def kernel(x_prompt, x_sample, c_prompt, c_sample, cache_k, cache_v, page_table, state_gla, w_ada, b_ada, g_pre_mix, w_in, b_sb, w_gate_up, b_gate, g_sb_out, g_gla_out, w_out, g_post_mix, g_pre_ffn, w_ffn_gate, w_ffn_up, w_ffn_down, g_post_ffn):
    batch, seq, d = x_prompt.shape
    nb = x_sample.shape[0]
    depth, n_pool, page, sb_heads, dh = cache_k.shape
    gla_heads, dk, dv = state_gla.shape[2:]
    rank = w_gate_up.shape[1]
    assert depth == 1 and x_sample.shape[1] == 1
    sbw, kw, vw = sb_heads * dh, gla_heads * dk, gla_heads * dv
    q_col, k_col, v_col, rg_col = 3 * sbw, 3 * sbw + kw, 3 * sbw + 2 * kw, 3 * sbw + 2 * kw + vw
    main = rg_col + vw

    row = lambda a: a.reshape(1, -1)
    uo = _suffix_sum_matrix()
    ones = jnp.ones((LANES, LANES), BF16)

    w_in0 = w_in[0]
    w_main = w_in0[:, :main].astype(BF16)
    w_low = jnp.pad(w_in0[:, main:], ((0, 0), (0, LANES - rank))).astype(BF16)
    wgu = jnp.pad(w_gate_up[0], ((0, LANES - rank), (0, 0))).astype(BF16)
    w_o = w_out[0].astype(BF16)
    w_fg, w_fu, w_fd = (w[0].astype(BF16) for w in (w_ffn_gate, w_ffn_up, w_ffn_down))

    n_c = batch + nb
    c_all = jnp.pad(jnp.concatenate([c_prompt, c_sample], axis=0), ((0, -n_c % SUBLANES), (0, 0)))
    mod = _ada(c_all, w_ada[0], row(b_ada[0]))
    mod_p = [a.reshape(batch, 1, d) for a in jnp.split(mod[:batch], 6, axis=-1)]
    mod_s = [a.reshape(1, nb, d) for a in jnp.split(mod[batch:n_c], 6, axis=-1)]

    gains = dict(g_sb=row(g_sb_out[0]), g_gla=row(g_gla_out[0]), g_post=row(g_post_mix[0]),
                 g_pre=row(g_pre_ffn[0]))

    def rest_of_layer(x2, osb, og, proj, mods, tm, rows_per_group, ffn_tm):
        sh1, sc1, gt1, sh2, sc2, gt2 = mods
        x1, h2 = _out_proj(osb, og, proj, rg_col, x2, gains["g_sb"], gains["g_gla"], w_o,
                           gains["g_post"], gt1, gains["g_pre"], sc2, sh2, tm, rows_per_group)
        return _ffn(h2, w_fg, w_fu, w_fd, x1, row(g_post_ffn[0]), gt2, ffn_tm, rows_per_group)

    xp = x_prompt.reshape(batch * seq, d)
    proj_p, glow_p = _in_proj(xp, row(g_pre_mix[0]), mod_p[1], mod_p[0], w_main, w_low,
                              tm=512, rows_per_group=seq)
    osb_p = _sb_prompt(proj_p, b_sb[0], uo, batch, seq, sb_heads, dh, tq=256)
    og_p, st_p = _gla_prompt(proj_p, glow_p, wgu, row(b_gate[0]), ones, batch, seq, gla_heads,
                             dk, dv, q_col, k_col, v_col)
    y_p = rest_of_layer(xp, osb_p, og_p, proj_p, mod_p, 256, seq, 512)

    xs = x_sample.reshape(nb, d)
    proj_s, glow_s = _in_proj(xs, row(g_pre_mix[0]), mod_s[1], mod_s[0], w_main, w_low,
                              tm=nb, rows_per_group=nb)
    bias_rows = jnp.broadcast_to(b_sb[0][:, None], (sb_heads, KEY_BLOCK))
    osb_s = _sb_decode(proj_s[:, :sbw].reshape(nb, sb_heads, dh),
                       cache_k[0].reshape(n_pool, page * sb_heads, dh),
                       cache_v[0].reshape(n_pool, page * sb_heads, dh),
                       page_table, bias_rows, uo)
    og_s, st_s = _gla_step(proj_s, glow_s, wgu, row(b_gate[0]), state_gla[0], dk, dv,
                           q_col, k_col, v_col)
    y_s = rest_of_layer(xs, osb_s.reshape(nb, sbw), og_s, proj_s, mod_s, nb, nb, nb)

    kv = lambda p, lo, n: p[:, lo:lo + sbw].reshape(1, n, -1, sb_heads, dh)
    return (y_p.reshape(batch, seq, d), y_s.reshape(nb, 1, d),
            kv(proj_p, sbw, batch), kv(proj_p, 2 * sbw, batch),
            jnp.swapaxes(st_p, -1, -2)[None],
            kv(proj_s, sbw, nb), kv(proj_s, 2 * sbw, nb), st_s[None])
```

```python
import functools

import jax
import jax.numpy as jnp
from jax import lax
from jax.experimental import pallas as pl
from jax.experimental.pallas import tpu as pltpu

F32 = jnp.float32
BF16 = jnp.bfloat16

LANES = 128
SUBLANES = 8
VMEM_LIMIT_BYTES = 52 * 1024 * 1024

EPS = 1e-6
GLA_GATE_NORM = 16.0
GLA_CHUNK = 64
GLA_SUB = 8
KEY_BLOCK = 128


def _params(*sem):
    return pltpu.CompilerParams(dimension_semantics=sem, vmem_limit_bytes=VMEM_LIMIT_BYTES)


def _log_sigmoid(z):
    return jnp.minimum(z, 0.0) - jnp.log(1.0 + jnp.exp(-jnp.abs(z)))


def _silu(x):
    return x * (1.0 / (1.0 + jnp.exp(-x)))


def _rms(x, g):
    ms = jnp.mean(x * x, axis=-1, keepdims=True)
    return x * lax.rsqrt(ms + EPS) * g


def _split_bf16(x, pieces):
    out = []
    for _ in range(pieces - 1):
        p = x.astype(BF16)
        out.append(p)
        x = x - p.astype(F32)
    out.append(x.astype(BF16))
    return out


def _dot(a, b):
    return jnp.dot(a, b, preferred_element_type=F32)


def _dot_nt(a, b):
    return lax.dot_general(a, b, (((1,), (1,)), ((), ())), preferred_element_type=F32)


def _dot_tn(a, b):
    return lax.dot_general(a, b, (((0,), (0,)), ((), ())), preferred_element_type=F32)


def _ada_kernel(c_ref, w_ref, b_ref, o_ref):
    a = _silu(c_ref[...]).astype(BF16)
    o_ref[...] = _dot(a, w_ref[...].astype(BF16)) + b_ref[...]


def _ada(c, w, b, tn=1024):
    m, d = c.shape
    n = w.shape[1]
    return pl.pallas_call(
        _ada_kernel,
        grid=(n // tn,),
        in_specs=[pl.BlockSpec((m, d), lambda j: (0, 0)),
                  pl.BlockSpec((d, tn), lambda j: (0, j)),
                  pl.BlockSpec((1, tn), lambda j: (0, j))],
        out_specs=pl.BlockSpec((m, tn), lambda j: (0, j)),
        out_shape=jax.ShapeDtypeStruct((m, n), F32),
        compiler_params=_params("arbitrary"),
        name="ada_mod",
    )(c, w, b)


def _in_proj_kernel(x_ref, g_ref, sc_ref, sh_ref, w_ref, wl_ref, o_ref, ol_ref, h_scr):
    @pl.when(pl.program_id(1) == 0)
    def _():
        h = _rms(x_ref[...], g_ref[...]) * (1.0 + sc_ref[...]) + sh_ref[...]
        h_scr[...] = h.astype(BF16)
        ol_ref[...] = _dot(h_scr[...], wl_ref[...])

    o_ref[...] = _dot(h_scr[...], w_ref[...])


def _in_proj(x, g, sc, sh, w, wl, tm, rows_per_group, tn=1024):
    m, d = x.shape
    n = w.shape[1]
    r = sc.shape[1]
    mod_spec = pl.BlockSpec((None, r, d), lambda i, j: ((i * tm) // rows_per_group, 0, 0))
    return pl.pallas_call(
        _in_proj_kernel,
        grid=(m // tm, n // tn),
        in_specs=[pl.BlockSpec((tm, d), lambda i, j: (i, 0)),
                  pl.BlockSpec((1, d), lambda i, j: (0, 0)),
                  mod_spec, mod_spec,
                  pl.BlockSpec((d, tn), lambda i, j: (0, j)),
                  pl.BlockSpec((d, LANES), lambda i, j: (0, 0))],
        out_specs=[pl.BlockSpec((tm, tn), lambda i, j: (i, j)),
                   pl.BlockSpec((tm, LANES), lambda i, j: (i, 0))],
        out_shape=[jax.ShapeDtypeStruct((m, n), F32),
                   jax.ShapeDtypeStruct((m, LANES), F32)],
        scratch_shapes=[pltpu.VMEM((tm, d), BF16)],
        compiler_params=_params("arbitrary", "arbitrary"),
        name="in_proj",
    )(x, g, sc, sh, w, wl)


def _sb_group(z, carry, uo, mask):
    n = z.shape[1] // KEY_BLOCK
    lb = _log_sigmoid(z)
    lk = lb - z
    if mask is not None:
        lk = jnp.where(mask, lk, 0.0)
    hi, lo = _split_bf16(lk, 2)
    tails = [None] * n
    for s in reversed(range(n)):
        sl = slice(s * KEY_BLOCK, (s + 1) * KEY_BLOCK)
        ts = _dot(jnp.concatenate([hi[:, sl], lo[:, sl]], axis=1), uo)
        tails[s] = ts[:, :KEY_BLOCK] + carry
        carry = carry + ts[:, KEY_BLOCK:]
    a = jnp.exp(lb + jnp.concatenate(tails, axis=1))
    if mask is not None:
        a = jnp.where(mask, a, 0.0)
    return a.astype(BF16), carry


def _sb_prompt_kernel(q_ref, k_ref, v_ref, bias_ref, uo_ref, o_ref, carry_scr, acc_scr, *,
                      tq, group, scale):
    qi = pl.program_id(2)
    h = pl.program_id(1)
    q = (q_ref[...] * scale).astype(BF16)
    bias = bias_ref[h]
    uo = uo_ref[...]
    n_diag = tq // KEY_BLOCK

    def visit(first_block, n, mask):
        rows = n * KEY_BLOCK
        start = pl.multiple_of(first_block * KEY_BLOCK, KEY_BLOCK)
        kb = k_ref[pl.ds(start, rows), :].astype(BF16)
        vb = v_ref[pl.ds(start, rows), :].astype(BF16)
        a, carry = _sb_group(_dot_nt(q, kb) + bias, carry_scr[...], uo, mask)
        carry_scr[...] = carry
        acc_scr[...] += _dot(a, vb)

    carry_scr[...] = jnp.zeros_like(carry_scr)
    acc_scr[...] = jnp.zeros_like(acc_scr)
    row = lax.broadcasted_iota(jnp.int32, (tq, tq), 0)
    col = lax.broadcasted_iota(jnp.int32, (tq, tq), 1)
    visit(qi * n_diag, n_diag, col < row)
    n_old = qi * n_diag
    n_groups = n_old // group

    def body(g, _):
        visit(n_old - (g + 1) * group, group, None)
        return 0

    lax.fori_loop(0, n_groups, body, 0)
    leftover = lax.rem(n_old, group)
    size = group // 2
    while size >= n_diag:
        pl.when(lax.rem(lax.div(leftover, size), 2) == 1)(
            functools.partial(visit, lax.rem(leftover, size), size, None))
        size //= 2
    o_ref[...] = acc_scr[...]


def _sb_prompt(proj, b_sb, uo, batch, seq, heads, dh, tq, group=4):
    m = proj.shape[0]
    nq = seq // tq
    kernel = functools.partial(_sb_prompt_kernel, tq=tq, group=group, scale=dh ** -0.5)
    return pl.pallas_call(
        kernel,
        grid=(batch, heads, nq),
        in_specs=[pl.BlockSpec((tq, dh), lambda b, h, i: (b * nq + i, h)),
                  pl.BlockSpec((seq, dh), lambda b, h, i: (b, heads + h)),
                  pl.BlockSpec((seq, dh), lambda b, h, i: (b, 2 * heads + h)),
                  pl.BlockSpec(memory_space=pltpu.SMEM),
                  pl.BlockSpec((2 * KEY_BLOCK, 2 * KEY_BLOCK), lambda b, h, i: (0, 0))],
        out_specs=pl.BlockSpec((tq, dh), lambda b, h, i: (b * nq + i, h)),
        out_shape=jax.ShapeDtypeStruct((m, heads * dh), F32),
        scratch_shapes=[pltpu.VMEM((tq, KEY_BLOCK), F32), pltpu.VMEM((tq, dh), F32)],
        compiler_params=_params("arbitrary", "arbitrary", "arbitrary"),
        name="sb_prompt",
    )(proj, proj, proj, b_sb, uo)


def _sb_decode_kernel(pt_ref, q_ref, bias_ref, uo_ref, *refs, pages, heads, scale):
    k_refs, v_refs = refs[:pages], refs[pages:2 * pages]
    o_ref, carry_scr, acc_scr = refs[2 * pages:]
    g = pl.program_id(1)

    @pl.when(g == 0)
    def _():
        carry_scr[...] = jnp.zeros_like(carry_scr)
        acc_scr[...] = jnp.zeros_like(acc_scr)

    q = (q_ref[...] * scale).astype(BF16)
    uo = uo_ref[...]
    width = pages * KEY_BLOCK

    def head_rows(refs_, h):
        return jnp.concatenate([r[pl.ds(h, KEY_BLOCK, stride=heads), :] for r in refs_],
                               axis=0).astype(BF16)

    head_of_row = lax.broadcasted_iota(jnp.int32, (heads, width), 0)
    z = jnp.zeros((heads, width), F32)
    for h in range(heads):
        z = jnp.where(head_of_row == h, _dot_nt(q, head_rows(k_refs, h)), z)
    zr = jnp.concatenate([z[:, p * KEY_BLOCK:(p + 1) * KEY_BLOCK] for p in range(pages)], axis=0)
    zr = zr + jnp.concatenate([bias_ref[...]] * pages, axis=0)
    lb = _log_sigmoid(zr)
    hi, lo = _split_bf16(lb - zr, 2)
    ts = _dot(jnp.concatenate([hi, lo], axis=1), uo)
    carry = carry_scr[...]
    tails = []
    for p in range(pages):
        tsp = ts[p * heads:(p + 1) * heads]
        tails.append(tsp[:, :KEY_BLOCK] + carry)
        carry = carry + tsp[:, KEY_BLOCK:]
    carry_scr[...] = carry
    ar = jnp.exp(lb + jnp.concatenate(tails, axis=0))
    a = jnp.concatenate([ar[p * heads:(p + 1) * heads] for p in range(pages)], axis=1).astype(BF16)
    head_of_out = lax.broadcasted_iota(jnp.int32, acc_scr.shape, 0)
    acc = acc_scr[...]
    for h in range(heads):
        acc = acc + jnp.where(head_of_out == h, _dot(a, head_rows(v_refs, h)), 0.0)
    acc_scr[...] = acc

    @pl.when(g == pl.num_programs(1) - 1)
    def _():
        o_ref[...] = acc


def _sb_decode(q, cache_k, cache_v, page_table, bias_rows, uo, pages=8):
    nb, heads, dh = q.shape
    n_pages = page_table.shape[1]
    rows = cache_k.shape[1]
    groups = n_pages // pages
    kernel = functools.partial(_sb_decode_kernel, pages=pages, heads=heads, scale=dh ** -0.5)

    def page_spec(i):
        return pl.BlockSpec((None, rows, dh),
                            lambda b, g, pt, i=i: (pt[b, n_pages - 1 - (g * pages + i)], 0, 0))

    grid_spec = pltpu.PrefetchScalarGridSpec(
        num_scalar_prefetch=1,
        grid=(nb, groups),
        in_specs=[pl.BlockSpec((None, heads, dh), lambda b, g, pt: (b, 0, 0)),
                  pl.BlockSpec((heads, KEY_BLOCK), lambda b, g, pt: (0, 0)),
                  pl.BlockSpec((2 * KEY_BLOCK, 2 * KEY_BLOCK), lambda b, g, pt: (0, 0))]
                 + [page_spec(i) for i in range(pages)] * 2,
        out_specs=pl.BlockSpec((None, heads, dh), lambda b, g, pt: (b, 0, 0)),
        scratch_shapes=[pltpu.VMEM((heads, KEY_BLOCK), F32), pltpu.VMEM((heads, dh), F32)],
    )
    return pl.pallas_call(
        kernel,
        grid_spec=grid_spec,
        out_shape=jax.ShapeDtypeStruct((nb, heads, dh), F32),
        compiler_params=_params("arbitrary", "arbitrary"),
        name="sb_decode",
    )(page_table, q, bias_rows, uo, *([cache_k] * pages), *([cache_v] * pages))


def _gla_prefix_matrix():
    t = lax.broadcasted_iota(jnp.int32, (GLA_CHUNK, GLA_CHUNK), 0)
    s = lax.broadcasted_iota(jnp.int32, (GLA_CHUNK, GLA_CHUNK), 1)
    return jnp.where(s <= t, 1.0, 0.0).astype(BF16)


def _gla_intra(qs, ks, bs, ones_bf16):
    n = len(qs)
    c = GLA_CHUNK
    t_i = lax.broadcasted_iota(jnp.int32, (c, c), 0)
    s_i = lax.broadcasted_iota(jnp.int32, (c, c), 1)
    atts = [jnp.zeros((c, c), F32)] * n
    h = c // 2
    while h >= GLA_SUB:
        th, sh = t_i // h, s_i // h
        pair = ((th - sh - 1) | ((th & 1) ^ 1)) == 0
        for ci in range(n):
            q, k, b = qs[ci], ks[ci], bs[ci]
            ref = jnp.concatenate(
                [jnp.broadcast_to(b[j + h - 1:j + h], (2 * h, b.shape[1])) for j in range(0, c, 2 * h)],
                axis=0)
            ql = (q * jnp.exp(jnp.minimum(b - ref, 0.0))).astype(BF16)
            kl = (k * jnp.exp(jnp.minimum(ref - b, 0.0))).astype(BF16)
            atts[ci] = atts[ci] + jnp.where(pair, _dot_nt(ql, kl), 0.0)
        h //= 2
    prods = []
    for ci in range(n):
        q, k, b = qs[ci], ks[ci], bs[ci]
        for lo_ in range(0, c, GLA_SUB):
            qi, ki, bi = (a[lo_:lo_ + GLA_SUB] for a in (q, k, b))
            prods += [qi * ki[s:s + 1] * jnp.exp(jnp.minimum(bi - bi[s:s + 1], 0.0))
                      for s in range(GLA_SUB)]
    rsum = _dot(jnp.concatenate(prods, axis=0).astype(BF16), ones_bf16)
    lane = lax.broadcasted_iota(jnp.int32, (GLA_SUB, LANES), 1)
    trow = lax.broadcasted_iota(jnp.int32, (GLA_SUB, LANES), 0)
    for ci in range(n):
        strips = []
        for i, lo_ in enumerate(range(0, c, GLA_SUB)):
            base = (ci * (c // GLA_SUB) + i) * GLA_SUB * GLA_SUB
            strip = jnp.zeros((GLA_SUB, LANES), F32)
            for s in range(GLA_SUB):
                strip = jnp.where(lane == lo_ + s,
                                  rsum[base + s * GLA_SUB:base + (s + 1) * GLA_SUB], strip)
            strips.append(jnp.where(lane - lo_ <= trow, strip, 0.0)[:, :c])
        atts[ci] = atts[ci] + jnp.concatenate(strips, axis=0)
    return atts


def _gla_prompt_kernel(q_ref, k_ref, v_ref, gl_ref, wg_ref, bg_ref, ones_ref, ll_ref, o_ref,
                       st_ref, st_scr, *, n_inner, scale):
    @pl.when(pl.program_id(2) == 0)
    def _():
        st_scr[...] = jnp.zeros_like(st_scr)

    c = GLA_CHUNK
    ll = ll_ref[...]
    g = _log_sigmoid(_dot(gl_ref[...].astype(BF16), wg_ref[...]) + bg_ref[...]) * (1.0 / GLA_GATE_NORM)
    chunks = [slice(ci * c, (ci + 1) * c) for ci in range(n_inner)]
    bs = [sum(_dot(ll, piece) for piece in _split_bf16(g[sl], 3)) for sl in chunks]
    qs = [q_ref[sl, :] * scale for sl in chunks]
    ks = [k_ref[sl, :] for sl in chunks]
    atts = _gla_intra(qs, ks, bs, ones_ref[...])
    st = st_scr[...]
    for ci, sl in enumerate(chunks):
        q, k, b = qs[ci], ks[ci], bs[ci]
        v16 = v_ref[sl, :].astype(BF16)
        o_ref[sl, :] = (_dot_nt((q * jnp.exp(b)).astype(BF16), st.astype(BF16))
                        + _dot(atts[ci].astype(BF16), v16))
        b_last = b[c - 1:c]
        kdec = (k * jnp.exp(b_last - b)).astype(BF16)
        st = st * jnp.exp(b_last) + _dot_tn(v16, kdec)
    st_scr[...] = st
    st_ref[...] = st


def _gla_prompt(proj, glow, wgu, b_gate, ones, batch, seq, heads, dk, dv, q_col, k_col, v_col,
                n_inner=4):
    m = proj.shape[0]
    cb = n_inner * GLA_CHUNK
    nc = seq // cb
    kernel = functools.partial(_gla_prompt_kernel, n_inner=n_inner, scale=dk ** -0.5)
    return pl.pallas_call(
        kernel,
        grid=(batch, heads, nc),
        in_specs=[pl.BlockSpec((cb, dk), lambda b, h, c: (b * nc + c, q_col // dk + h)),
                  pl.BlockSpec((cb, dk), lambda b, h, c: (b * nc + c, k_col // dk + h)),
                  pl.BlockSpec((cb, dv), lambda b, h, c: (b * nc + c, v_col // dv + h)),
                  pl.BlockSpec((cb, LANES), lambda b, h, c: (b * nc + c, 0)),
                  pl.BlockSpec((LANES, dk), lambda b, h, c: (0, h)),
                  pl.BlockSpec((1, dk), lambda b, h, c: (0, h)),
                  pl.BlockSpec((LANES, LANES), lambda b, h, c: (0, 0)),
                  pl.BlockSpec((GLA_CHUNK, GLA_CHUNK), lambda b, h, c: (0, 0))],
        out_specs=[pl.BlockSpec((cb, dv), lambda b, h, c: (b * nc + c, h)),
                   pl.BlockSpec((None, None, dv, dk), lambda b, h, c: (b, h, 0, 0))],
        out_shape=[jax.ShapeDtypeStruct((m, heads * dv), F32),
                   jax.ShapeDtypeStruct((batch, heads, dv, dk), F32)],
        scratch_shapes=[pltpu.VMEM((dv, dk), F32)],
        compiler_params=_params("arbitrary", "arbitrary", "arbitrary"),
        name="gla_prompt",
    )(proj, proj, proj, glow, wgu, b_gate, ones, _gla_prefix_matrix())


def _gla_step_kernel(q_ref, k_ref, v_ref, gl_ref, wg_ref, bg_ref, s_ref, o_ref, so_ref, *,
                     heads, dk, dv, scale):
    g = _log_sigmoid(_dot(gl_ref[...].astype(BF16), wg_ref[...]) + bg_ref[...]) * (1.0 / GLA_GATE_NORM)
    q = q_ref[...] * scale
    k = k_ref[...]
    v = v_ref[...]
    eye = (lax.broadcasted_iota(jnp.int32, (dk, dk), 0)
           == lax.broadcasted_iota(jnp.int32, (dk, dk), 1))

    def column(x_row):
        return jnp.sum(jnp.where(eye, x_row, 0.0), axis=1, keepdims=True)

    for h in range(heads):
        ksl = slice(h * dk, (h + 1) * dk)
        vsl = slice(h * dv, (h + 1) * dv)
        s_new = column(jnp.exp(g[:, ksl])) * s_ref[h] + column(k[:, ksl]) * v[:, vsl]
        so_ref[h] = s_new
        o_ref[:, vsl] = jnp.sum(column(q[:, ksl]) * s_new, axis=0, keepdims=True)


def _gla_step(proj, glow, wgu, b_gate, state, dk, dv, q_col, k_col, v_col):
    nb, heads = state.shape[0], state.shape[1]
    kw, vw = heads * dk, heads * dv
    kernel = functools.partial(_gla_step_kernel, heads=heads, dk=dk, dv=dv, scale=dk ** -0.5)
    proj3 = proj.reshape(nb, 1, proj.shape[1])
    glow3 = glow.reshape(nb, 1, LANES)
    o, s_new = pl.pallas_call(
        kernel,
        grid=(nb,),
        in_specs=[pl.BlockSpec((None, 1, kw), lambda b: (b, 0, q_col // kw)),
                  pl.BlockSpec((None, 1, kw), lambda b: (b, 0, k_col // kw)),
                  pl.BlockSpec((None, 1, vw), lambda b: (b, 0, v_col // vw)),
                  pl.BlockSpec((None, 1, LANES), lambda b: (b, 0, 0)),
                  pl.BlockSpec((LANES, kw), lambda b: (0, 0)),
                  pl.BlockSpec((1, kw), lambda b: (0, 0)),
                  pl.BlockSpec((None, heads, dk, dv), lambda b: (b, 0, 0, 0))],
        out_specs=[pl.BlockSpec((None, 1, vw), lambda b: (b, 0, 0)),
                   pl.BlockSpec((None, heads, dk, dv), lambda b: (b, 0, 0, 0))],
        out_shape=[jax.ShapeDtypeStruct((nb, 1, vw), F32),
                   jax.ShapeDtypeStruct(state.shape, F32)],
        compiler_params=_params("arbitrary"),
        name="gla_step",
    )(proj3, proj3, proj3, glow3, wgu, b_gate, state)
    return o.reshape(nb, vw), s_new


def _out_proj_kernel(osb_ref, og_ref, rg_ref, x_ref, gsb_ref, ggla_ref, w_ref, gpost_ref,
                     gt_ref, gpre_ref, sc_ref, sh_ref, x1_ref, h2_ref, mix_scr, *, dh, dv):
    sbw = osb_ref.shape[1]
    for h in range(sbw // dh):
        sl = slice(h * dh, (h + 1) * dh)
        mix_scr[:, sl] = _rms(osb_ref[:, sl], gsb_ref[...]).astype(BF16)
    for h in range(og_ref.shape[1] // dv):
        sl = slice(h * dv, (h + 1) * dv)
        y = _rms(og_ref[:, sl], ggla_ref[...]) * _silu(rg_ref[:, sl])
        mix_scr[:, sbw + h * dv:sbw + (h + 1) * dv] = y.astype(BF16)
    m = _dot(mix_scr[...], w_ref[...])
    x1 = x_ref[...] + gt_ref[...] * _rms(m, gpost_ref[...])
    x1_ref[...] = x1
    h2 = _rms(x1, gpre_ref[...]) * (1.0 + sc_ref[...]) + sh_ref[...]
    h2_ref[...] = h2.astype(BF16)


def _out_proj(osb, og, proj, rg_col, x, g_sb, g_gla, w, g_post, gt, g_pre, sc, sh, tm,
              rows_per_group):
    m, d = x.shape
    sbw, gw = osb.shape[1], og.shape[1]
    dh, dv = g_sb.shape[1], g_gla.shape[1]
    r = gt.shape[1]
    mod_spec = pl.BlockSpec((None, r, d), lambda i: ((i * tm) // rows_per_group, 0, 0))
    row = lambda width: pl.BlockSpec((1, width), lambda i: (0, 0))
    kernel = functools.partial(_out_proj_kernel, dh=dh, dv=dv)
    return pl.pallas_call(
        kernel,
        grid=(m // tm,),
        in_specs=[pl.BlockSpec((tm, sbw), lambda i: (i, 0)),
                  pl.BlockSpec((tm, gw), lambda i: (i, 0)),
                  pl.BlockSpec((tm, gw), lambda i: (i, rg_col // gw)),
                  pl.BlockSpec((tm, d), lambda i: (i, 0)),
                  row(dh), row(dv),
                  pl.BlockSpec((sbw + gw, d), lambda i: (0, 0)),
                  row(d), mod_spec, row(d), mod_spec, mod_spec],
        out_specs=[pl.BlockSpec((tm, d), lambda i: (i, 0)),
                   pl.BlockSpec((tm, d), lambda i: (i, 0))],
        out_shape=[jax.ShapeDtypeStruct((m, d), F32),
                   jax.ShapeDtypeStruct((m, d), BF16)],
        scratch_shapes=[pltpu.VMEM((tm, sbw + gw), BF16)],
        compiler_params=_params("arbitrary"),
        name="out_proj",
    )(osb, og, proj, x, g_sb, g_gla, w, g_post, gt, g_pre, sc, sh)


def _ffn_kernel(h_ref, wg_ref, wu_ref, wd_ref, x1_ref, gpost_ref, gt_ref, y_ref, acc_scr):
    j = pl.program_id(1)

    @pl.when(j == 0)
    def _():
        acc_scr[...] = jnp.zeros_like(acc_scr)

    h = h_ref[...]
    t = _silu(_dot(h, wg_ref[...])) * _dot(h, wu_ref[...])
    acc_scr[...] += _dot(t.astype(BF16), wd_ref[...])

    @pl.when(j == pl.num_programs(1) - 1)
    def _():
        y_ref[...] = x1_ref[...] + gt_ref[...] * _rms(acc_scr[...], gpost_ref[...])


def _ffn(h2, wg, wu, wd, x1, g_post, gt, tm, rows_per_group, tf=512):
    m, d = x1.shape
    f = wg.shape[1]
    r = gt.shape[1]
    return pl.pallas_call(
        _ffn_kernel,
        grid=(m // tm, f // tf),
        in_specs=[pl.BlockSpec((tm, d), lambda i, j: (i, 0)),
                  pl.BlockSpec((d, tf), lambda i, j: (0, j)),
                  pl.BlockSpec((d, tf), lambda i, j: (0, j)),
                  pl.BlockSpec((tf, d), lambda i, j: (j, 0)),
                  pl.BlockSpec((tm, d), lambda i, j: (i, 0)),
                  pl.BlockSpec((1, d), lambda i, j: (0, 0)),
                  pl.BlockSpec((None, r, d), lambda i, j: ((i * tm) // rows_per_group, 0, 0))],
        out_specs=pl.BlockSpec((tm, d), lambda i, j: (i, 0)),
        out_shape=jax.ShapeDtypeStruct((m, d), F32),
        scratch_shapes=[pltpu.VMEM((tm, d), F32)],
        compiler_params=_params("arbitrary", "arbitrary"),
        name="ffn",
    )(h2, wg, wu, wd, x1, g_post, gt)


def _suffix_sum_matrix():
    j = lax.broadcasted_iota(jnp.int32, (2 * KEY_BLOCK, 2 * KEY_BLOCK), 0) % KEY_BLOCK
    s = lax.broadcasted_iota(jnp.int32, (2 * KEY_BLOCK, 2 * KEY_BLOCK), 1)
    return jnp.where((j > s) | (s >= KEY_BLOCK), 1.0, 0.0).astype(BF16)


def kernel(x_prompt, x_sample, c_prompt, c_sample, cache_k, cache_v, page_table, state_gla, w_ada, b_ada, g_pre_mix, w_in, b_sb, w_gate_up, b_gate, g_sb_out, g_gla_out, w_out, g_post_mix, g_pre_ffn, w_ffn_gate, w_ffn_up, w_ffn_down, g_post_ffn):
    batch, seq, d = x_prompt.shape
    nb = x_sample.shape[0]
    depth, n_pool, page, sb_heads, dh = cache_k.shape
    gla_heads, dk, dv = state_gla.shape[2:]
    rank = w_gate_up.shape[1]
    assert depth == 1 and x_sample.shape[1] == 1
    sbw, kw, vw = sb_heads * dh, gla_heads * dk, gla_heads * dv
    q_col, k_col, v_col, rg_col = 3 * sbw, 3 * sbw + kw, 3 * sbw + 2 * kw, 3 * sbw + 2 * kw + vw
    main = rg_col + vw

    row = lambda a: a.reshape(1, -1)
    uo = _suffix_sum_matrix()
    ones = jnp.ones((LANES, LANES), BF16)

    w_in0 = w_in[0]
    w_main = w_in0[:, :main].astype(BF16)
    w_low = jnp.pad(w_in0[:, main:], ((0, 0), (0, LANES - rank))).astype(BF16)
    wgu = jnp.pad(w_gate_up[0], ((0, LANES - rank), (0, 0))).astype(BF16)
    w_o = w_out[0].astype(BF16)
    w_fg, w_fu, w_fd = (w[0].astype(BF16) for w in (w_ffn_gate, w_ffn_up, w_ffn_down))

    n_c = batch + nb
    c_all = jnp.pad(jnp.concatenate([c_prompt, c_sample], axis=0), ((0, -n_c % SUBLANES), (0, 0)))
    mod = _ada(c_all, w_ada[0], row(b_ada[0]))
    mod_p = [a.reshape(batch, 1, d) for a in jnp.split(mod[:batch], 6, axis=-1)]
    mod_s = [a.reshape(1, nb, d) for a in jnp.split(mod[batch:n_c], 6, axis=-1)]

    gains = dict(g_sb=row(g_sb_out[0]), g_gla=row(g_gla_out[0]), g_post=row(g_post_mix[0]),
                 g_pre=row(g_pre_ffn[0]))

    def rest_of_layer(x2, osb, og, proj, mods, tm, rows_per_group, ffn_tm):
        sh1, sc1, gt1, sh2, sc2, gt2 = mods
        x1, h2 = _out_proj(osb, og, proj, rg_col, x2, gains["g_sb"], gains["g_gla"], w_o,
                           gains["g_post"], gt1, gains["g_pre"], sc2, sh2, tm, rows_per_group)
        return _ffn(h2, w_fg, w_fu, w_fd, x1, row(g_post_ffn[0]), gt2, ffn_tm, rows_per_group)

    xp = x_prompt.reshape(batch * seq, d)
    proj_p, glow_p = _in_proj(xp, row(g_pre_mix[0]), mod_p[1], mod_p[0], w_main, w_low,
                              tm=512, rows_per_group=seq)
    osb_p = _sb_prompt(proj_p, b_sb[0], uo, batch, seq, sb_heads, dh, tq=256)
    og_p, st_p = _gla_prompt(proj_p, glow_p, wgu, row(b_gate[0]), ones, batch, seq, gla_heads,
                             dk, dv, q_col, k_col, v_col)
    y_p = rest_of_layer(xp, osb_p, og_p, proj_p, mod_p, 256, seq, 512)

    xs = x_sample.reshape(nb, d)
    proj_s, glow_s = _in_proj(xs, row(g_pre_mix[0]), mod_s[1], mod_s[0], w_main, w_low,
                              tm=nb, rows_per_group=nb)
    bias_rows = jnp.broadcast_to(b_sb[0][:, None], (sb_heads, KEY_BLOCK))
    osb_s = _sb_decode(proj_s[:, :sbw].reshape(nb, sb_heads, dh),
                       cache_k[0].reshape(n_pool, page * sb_heads, dh),
                       cache_v[0].reshape(n_pool, page * sb_heads, dh),
                       page_table, bias_rows, uo)
    og_s, st_s = _gla_step(proj_s, glow_s, wgu, row(b_gate[0]), state_gla[0], dk, dv,
                           q_col, k_col, v_col)
    y_s = rest_of_layer(xs, osb_s.reshape(nb, sbw), og_s, proj_s, mod_s, nb, nb, nb)

    kv = lambda p, lo, n: p[:, lo:lo + sbw].reshape(1, n, -1, sb_heads, dh)
    return (y_p.reshape(batch, seq, d), y_s.reshape(nb, 1, d),
            kv(proj_p, sbw, batch), kv(proj_p, 2 * sbw, batch),
            jnp.swapaxes(st_p, -1, -2)[None],
            kv(proj_s, sbw, nb), kv(proj_s, 2 * sbw, nb), st_s[None])
```

```python
import functools

import jax
import jax.numpy as jnp
from jax import lax
from jax.experimental import pallas as pl
from jax.experimental.pallas import tpu as pltpu

F32 = jnp.float32
BF16 = jnp.bfloat16

LANES = 128
SUBLANES = 8
VMEM_LIMIT_BYTES = 52 * 1024 * 1024

EPS = 1e-6
GLA_GATE_NORM = 16.0
GLA_CHUNK = 64
GLA_SUB = 8
KEY_BLOCK = 128


def _params(*sem):
    return pltpu.CompilerParams(dimension_semantics=sem, vmem_limit_bytes=VMEM_LIMIT_BYTES)


def _log_sigmoid(z):
    return jnp.minimum(z, 0.0) - jnp.log(1.0 + jnp.exp(-jnp.abs(z)))


def _silu(x):
    return x * (1.0 / (1.0 + jnp.exp(-x)))


def _rms(x, g):
    ms = jnp.mean(x * x, axis=-1, keepdims=True)
    return x * lax.rsqrt(ms + EPS) * g


def _split_bf16(x, pieces):
    out = []
    for _ in range(pieces - 1):
        p = x.astype(BF16)
        out.append(p)
        x = x - p.astype(F32)
    out.append(x.astype(BF16))
    return out


def _dot(a, b):
    return jnp.dot(a, b, preferred_element_type=F32)


def _dot_nt(a, b):
    return lax.dot_general(a, b, (((1,), (1,)), ((), ())), preferred_element_type=F32)


def _dot_tn(a, b):
    return lax.dot_general(a, b, (((0,), (0,)), ((), ())), preferred_element_type=F32)


def _ada_kernel(c_ref, w_ref, b_ref, o_ref):
    a = _silu(c_ref[...]).astype(BF16)
    o_ref[...] = _dot(a, w_ref[...].astype(BF16)) + b_ref[...]


def _ada(c, w, b, tn=1024):
    m, d = c.shape
    n = w.shape[1]
    return pl.pallas_call(
        _ada_kernel,
        grid=(n // tn,),
        in_specs=[pl.BlockSpec((m, d), lambda j: (0, 0)),
                  pl.BlockSpec((d, tn), lambda j: (0, j)),
                  pl.BlockSpec((1, tn), lambda j: (0, j))],
        out_specs=pl.BlockSpec((m, tn), lambda j: (0, j)),
        out_shape=jax.ShapeDtypeStruct((m, n), F32),
        compiler_params=_params("arbitrary"),
        name="ada_mod",
    )(c, w, b)


def _in_proj_kernel(x_ref, g_ref, sc_ref, sh_ref, w_ref, wl_ref, qkv_ref, krow_ref, vrow_ref,
                    gla_ref, ol_ref, h_scr, *, heads, dh, q_scale):
    j = pl.program_id(1)

    @pl.when(j == 0)
    def _():
        h = _rms(x_ref[...], g_ref[...]) * (1.0 + sc_ref[...]) + sh_ref[...]
        h_scr[...] = h.astype(BF16)
        ol_ref[...] = _dot(h_scr[...], wl_ref[...])

    res = _dot(h_scr[...], w_ref[...])

    @pl.when(j == 0)
    def _():
        qkv_ref[...] = (res * q_scale).astype(BF16)

    for tile, rows_ref in ((1, krow_ref), (2, vrow_ref)):
        @pl.when(j == tile)
        def _(rows_ref=rows_ref):
            qkv_ref[...] = res.astype(BF16)
            for h in range(heads):
                rows_ref[pl.ds(h, res.shape[0], stride=heads), :] = res[:, h * dh:(h + 1) * dh]

    @pl.when(j >= 3)
    def _():
        gla_ref[...] = res


def _in_proj(x, g, sc, sh, w, wl, tm, rows_per_group, heads, dh):
    m, d = x.shape
    tn = heads * dh
    r = sc.shape[1]
    mod_spec = pl.BlockSpec((None, r, d), lambda i, j: ((i * tm) // rows_per_group, 0, 0))
    kernel = functools.partial(_in_proj_kernel, heads=heads, dh=dh, q_scale=dh ** -0.5)
    return pl.pallas_call(
        kernel,
        grid=(m // tm, 6),
        in_specs=[pl.BlockSpec((tm, d), lambda i, j: (i, 0)),
                  pl.BlockSpec((1, d), lambda i, j: (0, 0)),
                  mod_spec, mod_spec,
                  pl.BlockSpec((d, tn), lambda i, j: (0, j)),
                  pl.BlockSpec((d, LANES), lambda i, j: (0, 0))],
        out_specs=[pl.BlockSpec((None, tm, tn), lambda i, j: (jnp.minimum(j, 2), i, 0)),
                   pl.BlockSpec((tm * heads, dh), lambda i, j: (i, 0)),
                   pl.BlockSpec((tm * heads, dh), lambda i, j: (i, 0)),
                   pl.BlockSpec((tm, tn), lambda i, j: (i, jnp.clip(j - 3, 0, 2))),
                   pl.BlockSpec((tm, LANES), lambda i, j: (i, 0))],
        out_shape=[jax.ShapeDtypeStruct((3, m, tn), BF16),
                   jax.ShapeDtypeStruct((m * heads, dh), F32),
                   jax.ShapeDtypeStruct((m * heads, dh), F32),
                   jax.ShapeDtypeStruct((m, 3 * tn), F32),
                   jax.ShapeDtypeStruct((m, LANES), F32)],
        scratch_shapes=[pltpu.VMEM((tm, d), BF16)],
        compiler_params=_params("arbitrary", "arbitrary"),
        name="in_proj",
    )(x, g, sc, sh, w, wl)


def _sb_group(z, carry, uo, mask):
    n = z.shape[1] // KEY_BLOCK
    lb = _log_sigmoid(z)
    lk = lb - z
    if mask is not None:
        lk = jnp.where(mask, lk, 0.0)
    hi, lo = _split_bf16(lk, 2)
    tails = [None] * n
    for s in reversed(range(n)):
        sl = slice(s * KEY_BLOCK, (s + 1) * KEY_BLOCK)
        ts = _dot(jnp.concatenate([hi[:, sl], lo[:, sl]], axis=1), uo)
        tails[s] = ts[:, :KEY_BLOCK] + carry
        carry = carry + ts[:, KEY_BLOCK:]
    a = jnp.exp(lb + jnp.concatenate(tails, axis=1))
    if mask is not None:
        a = jnp.where(mask, a, 0.0)
    return a.astype(BF16), carry


def _sb_prompt_kernel(q_ref, k_ref, v_ref, bias_ref, uo_ref, o_ref, carry_scr, acc_scr, *,
                      tq, group):
    qi = pl.program_id(2)
    h = pl.program_id(1)
    q = q_ref[...]
    bias = bias_ref[h]
    uo = uo_ref[...]
    n_diag = tq // KEY_BLOCK

    def visit(first_block, n, newest):
        rows = n * KEY_BLOCK
        start = pl.multiple_of(first_block * KEY_BLOCK, KEY_BLOCK)
        kb = k_ref[pl.ds(start, rows), :]
        vb = v_ref[pl.ds(start, rows), :]
        mask = None
        if newest:
            mask = (lax.broadcasted_iota(jnp.int32, (tq, rows), 1) + (start - qi * tq)
                    < lax.broadcasted_iota(jnp.int32, (tq, rows), 0))
        a, carry = _sb_group(_dot_nt(q, kb) + bias, carry_scr[...], uo, mask)
        carry_scr[...] = carry
        acc_scr[...] += _dot(a, vb)

    carry_scr[...] = jnp.zeros_like(carry_scr)
    acc_scr[...] = jnp.zeros_like(acc_scr)
    n_tot = (qi + 1) * n_diag
    n_groups = n_tot // group
    leftover = lax.rem(n_tot, group)

    @pl.when(n_groups > 0)
    def _():
        visit(n_tot - group, group, True)

    def body(g, _):
        visit(n_tot - (g + 1) * group, group, False)
        return 0

    lax.fori_loop(1, n_groups, body, 0)
    size = group // 2
    while size >= n_diag:
        take = lax.rem(lax.div(leftover, size), 2) == 1
        is_first = jnp.logical_and(n_groups == 0, lax.div(leftover, 2 * size) == 0)
        for newest in (True, False):
            pl.when(jnp.logical_and(take, is_first == newest))(
                functools.partial(visit, lax.rem(leftover, size), size, newest))
        size //= 2
    o_ref[...] = acc_scr[...]


def _sb_prompt(qkv, b_sb, uo, batch, seq, heads, dh, tq, group=8):
    m = qkv.shape[1]
    nq = seq // tq
    kernel = functools.partial(_sb_prompt_kernel, tq=tq, group=group)
    return pl.pallas_call(
        kernel,
        grid=(batch, heads, nq),
        in_specs=[pl.BlockSpec((None, tq, dh), lambda b, h, i: (0, b * nq + i, h)),
                  pl.BlockSpec((None, seq, dh), lambda b, h, i: (1, b, h)),
                  pl.BlockSpec((None, seq, dh), lambda b, h, i: (2, b, h)),
                  pl.BlockSpec(memory_space=pltpu.SMEM),
                  pl.BlockSpec((2 * KEY_BLOCK, 2 * KEY_BLOCK), lambda b, h, i: (0, 0))],
        out_specs=pl.BlockSpec((tq, dh), lambda b, h, i: (b * nq + i, h)),
        out_shape=jax.ShapeDtypeStruct((m, heads * dh), F32),
        scratch_shapes=[pltpu.VMEM((tq, KEY_BLOCK), F32), pltpu.VMEM((tq, dh), F32)],
        compiler_params=_params("arbitrary", "arbitrary", "arbitrary"),
        name="sb_prompt",
    )(qkv, qkv, qkv, b_sb, uo)


def _sb_decode_kernel(pt_ref, q_ref, bias_ref, uo_ref, gather_ref, spread_ref, *refs, pages, heads):
    k_refs, v_refs = refs[:pages], refs[pages:2 * pages]
    o_ref, carry_scr, acc_scr = refs[2 * pages:]
    g = pl.program_id(1)

    @pl.when(g == 0)
    def _():
        carry_scr[...] = jnp.zeros_like(carry_scr)
        acc_scr[...] = jnp.zeros_like(acc_scr)

    q = q_ref[...]
    rows = k_refs[0].shape[0]
    own = (lax.broadcasted_iota(jnp.int32, (pages * heads, rows), 1) % heads
           == lax.broadcasted_iota(jnp.int32, (pages * heads, rows), 0) % heads)
    z_all = jnp.concatenate([_dot_nt(q, r[...].astype(BF16)) for r in k_refs], axis=0)
    pieces = _split_bf16(jnp.where(own, z_all, 0.0), 3)
    zr = _dot(jnp.concatenate(pieces, axis=1), gather_ref[...])
    zr = zr + jnp.concatenate([bias_ref[...]] * pages, axis=0)
    lb = _log_sigmoid(zr)
    hi, lo = _split_bf16(lb - zr, 2)
    ts = _dot(jnp.concatenate([hi, lo], axis=1), uo_ref[...])
    carry = carry_scr[...]
    tails = []
    for p in range(pages):
        tsp = ts[p * heads:(p + 1) * heads]
        tails.append(tsp[:, :KEY_BLOCK] + carry)
        carry = carry + tsp[:, KEY_BLOCK:]
    carry_scr[...] = carry
    a = jnp.exp(lb + jnp.concatenate(tails, axis=0)).astype(BF16)
    a_rows = jnp.where(own, _dot(a, spread_ref[...]), 0.0)
    acc = acc_scr[...]
    for p in range(pages):
        acc = acc + _dot(a_rows[p * heads:(p + 1) * heads].astype(BF16), v_refs[p][...].astype(BF16))
    acc_scr[...] = acc

    @pl.when(g == pl.num_programs(1) - 1)
    def _():
        o_ref[...] = acc


def _sb_decode(q, cache_k, cache_v, page_table, bias_rows, uo, pages=8):
    nb, heads, dh = q.shape
    n_pages = page_table.shape[1]
    rows = cache_k.shape[1]
    groups = n_pages // pages
    assert rows == KEY_BLOCK * heads
    kernel = functools.partial(_sb_decode_kernel, pages=pages, heads=heads)
    key_of_row = lax.broadcasted_iota(jnp.int32, (rows, KEY_BLOCK), 0) // heads
    gather1 = jnp.where(key_of_row == lax.broadcasted_iota(jnp.int32, (rows, KEY_BLOCK), 1), 1.0, 0.0)
    gather = jnp.concatenate([gather1] * 3, axis=0).astype(BF16)
    spread = gather1.T.astype(BF16)

    def page_spec(i):
        return pl.BlockSpec((None, rows, dh),
                            lambda b, g, pt, i=i: (pt[b, n_pages - 1 - (g * pages + i)], 0, 0))

    const = lambda a: pl.BlockSpec(a.shape, lambda b, g, pt: (0, 0))
    grid_spec = pltpu.PrefetchScalarGridSpec(
        num_scalar_prefetch=1,
        grid=(nb, groups),
        in_specs=[pl.BlockSpec((None, heads, dh), lambda b, g, pt: (b, 0, 0)),
                  const(bias_rows), const(uo), const(gather), const(spread)]
                 + [page_spec(i) for i in range(pages)] * 2,
        out_specs=pl.BlockSpec((None, heads, dh), lambda b, g, pt: (b, 0, 0)),
        scratch_shapes=[pltpu.VMEM((heads, KEY_BLOCK), F32), pltpu.VMEM((heads, dh), F32)],
    )
    return pl.pallas_call(
        kernel,
        grid_spec=grid_spec,
        out_shape=jax.ShapeDtypeStruct((nb, heads, dh), F32),
        compiler_params=_params("arbitrary", "arbitrary"),
        name="sb_decode",
    )(page_table, q, bias_rows, uo, gather, spread, *([cache_k] * pages), *([cache_v] * pages))


def _gla_prefix_matrix():
    t = lax.broadcasted_iota(jnp.int32, (GLA_CHUNK, GLA_CHUNK), 0)
    s = lax.broadcasted_iota(jnp.int32, (GLA_CHUNK, GLA_CHUNK), 1)
    return jnp.where(s <= t, 1.0, 0.0).astype(BF16)


def _gla_intra(qs, ks, bs, ones_bf16):
    n = len(qs)
    c = GLA_CHUNK
    t_i = lax.broadcasted_iota(jnp.int32, (c, c), 0)
    s_i = lax.broadcasted_iota(jnp.int32, (c, c), 1)
    atts = [jnp.zeros((c, c), F32)] * n
    h = c // 2
    while h >= GLA_SUB:
        th, sh = t_i // h, s_i // h
        pair = ((th - sh - 1) | ((th & 1) ^ 1)) == 0
        for ci in range(n):
            q, k, b = qs[ci], ks[ci], bs[ci]
            ref = jnp.concatenate(
                [jnp.broadcast_to(b[j + h - 1:j + h], (2 * h, b.shape[1])) for j in range(0, c, 2 * h)],
                axis=0)
            ql = (q * jnp.exp(jnp.minimum(b - ref, 0.0))).astype(BF16)
            kl = (k * jnp.exp(jnp.minimum(ref - b, 0.0))).astype(BF16)
            atts[ci] = atts[ci] + jnp.where(pair, _dot_nt(ql, kl), 0.0)
        h //= 2
    prods = []
    for ci in range(n):
        q, k, b = qs[ci], ks[ci], bs[ci]
        for lo_ in range(0, c, GLA_SUB):
            qi, ki, bi = (a[lo_:lo_ + GLA_SUB] for a in (q, k, b))
            prods += [qi * ki[s:s + 1] * jnp.exp(jnp.minimum(bi - bi[s:s + 1], 0.0))
                      for s in range(GLA_SUB)]
    rsum = _dot(jnp.concatenate(prods, axis=0).astype(BF16), ones_bf16)
    lane = lax.broadcasted_iota(jnp.int32, (GLA_SUB, LANES), 1)
    trow = lax.broadcasted_iota(jnp.int32, (GLA_SUB, LANES), 0)
    for ci in range(n):
        strips = []
        for i, lo_ in enumerate(range(0, c, GLA_SUB)):
            base = (ci * (c // GLA_SUB) + i) * GLA_SUB * GLA_SUB
            strip = jnp.zeros((GLA_SUB, LANES), F32)
            for s in range(GLA_SUB):
                strip = jnp.where(lane == lo_ + s,
                                  rsum[base + s * GLA_SUB:base + (s + 1) * GLA_SUB], strip)
            strips.append(jnp.where(lane - lo_ <= trow, strip, 0.0)[:, :c])
        atts[ci] = atts[ci] + jnp.concatenate(strips, axis=0)
    return atts


def _gla_prompt_kernel(q_ref, k_ref, v_ref, gl_ref, wg_ref, bg_ref, ones_ref, ll_ref, o_ref,
                       st_ref, st_scr, *, n_inner, scale):
    @pl.when(pl.program_id(2) == 0)
    def _():
        st_scr[...] = jnp.zeros_like(st_scr)

    c = GLA_CHUNK
    ll = ll_ref[...]
    g = _log_sigmoid(_dot(gl_ref[...].astype(BF16), wg_ref[...]) + bg_ref[...]) * (1.0 / GLA_GATE_NORM)
    chunks = [slice(ci * c, (ci + 1) * c) for ci in range(n_inner)]
    bs = [sum(_dot(ll, piece) for piece in _split_bf16(g[sl], 3)) for sl in chunks]
    qs = [q_ref[sl, :] * scale for sl in chunks]
    ks = [k_ref[sl, :] for sl in chunks]
    atts = _gla_intra(qs, ks, bs, ones_ref[...])
    st = st_scr[...]
    for ci, sl in enumerate(chunks):
        q, k, b = qs[ci], ks[ci], bs[ci]
        v16 = v_ref[sl, :].astype(BF16)
        o_ref[sl, :] = (_dot_nt((q * jnp.exp(b)).astype(BF16), st.astype(BF16))
                        + _dot(atts[ci].astype(BF16), v16))
        b_last = b[c - 1:c]
        kdec = (k * jnp.exp(b_last - b)).astype(BF16)
        st = st * jnp.exp(b_last) + _dot_tn(v16, kdec)
    st_scr[...] = st
    st_ref[...] = st


def _gla_prompt(proj, glow, wgu, b_gate, ones, batch, seq, heads, dk, dv, q_col, k_col, v_col,
                n_inner=4):
    m = proj.shape[0]
    cb = n_inner * GLA_CHUNK
    nc = seq // cb
    kernel = functools.partial(_gla_prompt_kernel, n_inner=n_inner, scale=dk ** -0.5)
    return pl.pallas_call(
        kernel,
        grid=(batch, heads, nc),
        in_specs=[pl.BlockSpec((cb, dk), lambda b, h, c: (b * nc + c, q_col // dk + h)),
                  pl.BlockSpec((cb, dk), lambda b, h, c: (b * nc + c, k_col // dk + h)),
                  pl.BlockSpec((cb, dv), lambda b, h, c: (b * nc + c, v_col // dv + h)),
                  pl.BlockSpec((cb, LANES), lambda b, h, c: (b * nc + c, 0)),
                  pl.BlockSpec((LANES, dk), lambda b, h, c: (0, h)),
                  pl.BlockSpec((1, dk), lambda b, h, c: (0, h)),
                  pl.BlockSpec((LANES, LANES), lambda b, h, c: (0, 0)),
                  pl.BlockSpec((GLA_CHUNK, GLA_CHUNK), lambda b, h, c: (0, 0))],
        out_specs=[pl.BlockSpec((cb, dv), lambda b, h, c: (b * nc + c, h)),
                   pl.BlockSpec((None, None, dv, dk), lambda b, h, c: (b, h, 0, 0))],
        out_shape=[jax.ShapeDtypeStruct((m, heads * dv), F32),
                   jax.ShapeDtypeStruct((batch, heads, dv, dk), F32)],
        scratch_shapes=[pltpu.VMEM((dv, dk), F32)],
        compiler_params=_params("arbitrary", "arbitrary", "arbitrary"),
        name="gla_prompt",
    )(proj, proj, proj, glow, wgu, b_gate, ones, _gla_prefix_matrix())


def _gla_step_kernel(q_ref, k_ref, v_ref, gl_ref, wg_ref, bg_ref, s_ref, o_ref, so_ref, *,
                     heads, dk, dv, scale):
    g = _log_sigmoid(_dot(gl_ref[...].astype(BF16), wg_ref[...]) + bg_ref[...]) * (1.0 / GLA_GATE_NORM)
    q = q_ref[...] * scale
    k = k_ref[...]
    v = v_ref[...]
    eye = (lax.broadcasted_iota(jnp.int32, (dk, dk), 0)
           == lax.broadcasted_iota(jnp.int32, (dk, dk), 1))

    def column(x_row):
        return jnp.sum(jnp.where(eye, x_row, 0.0), axis=1, keepdims=True)

    for h in range(heads):
        ksl = slice(h * dk, (h + 1) * dk)
        vsl = slice(h * dv, (h + 1) * dv)
        s_new = column(jnp.exp(g[:, ksl])) * s_ref[h] + column(k[:, ksl]) * v[:, vsl]
        so_ref[h] = s_new
        o_ref[:, vsl] = jnp.sum(column(q[:, ksl]) * s_new, axis=0, keepdims=True)


def _gla_step(proj, glow, wgu, b_gate, state, dk, dv, q_col, k_col, v_col):
    nb, heads = state.shape[0], state.shape[1]
    kw, vw = heads * dk, heads * dv
    kernel = functools.partial(_gla_step_kernel, heads=heads, dk=dk, dv=dv, scale=dk ** -0.5)
    proj3 = proj.reshape(nb, 1, proj.shape[1])
    glow3 = glow.reshape(nb, 1, LANES)
    o, s_new = pl.pallas_call(
        kernel,
        grid=(nb,),
        in_specs=[pl.BlockSpec((None, 1, kw), lambda b: (b, 0, q_col // kw)),
                  pl.BlockSpec((None, 1, kw), lambda b: (b, 0, k_col // kw)),
                  pl.BlockSpec((None, 1, vw), lambda b: (b, 0, v_col // vw)),
                  pl.BlockSpec((None, 1, LANES), lambda b: (b, 0, 0)),
                  pl.BlockSpec((LANES, kw), lambda b: (0, 0)),
                  pl.BlockSpec((1, kw), lambda b: (0, 0)),
                  pl.BlockSpec((None, heads, dk, dv), lambda b: (b, 0, 0, 0))],
        out_specs=[pl.BlockSpec((None, 1, vw), lambda b: (b, 0, 0)),
                   pl.BlockSpec((None, heads, dk, dv), lambda b: (b, 0, 0, 0))],
        out_shape=[jax.ShapeDtypeStruct((nb, 1, vw), F32),
                   jax.ShapeDtypeStruct(state.shape, F32)],
        compiler_params=_params("arbitrary"),
        name="gla_step",
    )(proj3, proj3, proj3, glow3, wgu, b_gate, state)
    return o.reshape(nb, vw), s_new


def _out_proj_kernel(osb_ref, og_ref, rg_ref, x_ref, gsb_ref, ggla_ref, w_ref, gpost_ref,
                     gt_ref, gpre_ref, sc_ref, sh_ref, x1_ref, h2_ref, mix_scr, *, dh, dv):
    sbw = osb_ref.shape[1]
    for h in range(sbw // dh):
        sl = slice(h * dh, (h + 1) * dh)
        mix_scr[:, sl] = _rms(osb_ref[:, sl], gsb_ref[...]).astype(BF16)
    for h in range(og_ref.shape[1] // dv):
        sl = slice(h * dv, (h + 1) * dv)
        y = _rms(og_ref[:, sl], ggla_ref[...]) * _silu(rg_ref[:, sl])
        mix_scr[:, sbw + h * dv:sbw + (h + 1) * dv] = y.astype(BF16)
    m = _dot(mix_scr[...], w_ref[...])
    x1 = x_ref[...] + gt_ref[...] * _rms(m, gpost_ref[...])
    x1_ref[...] = x1
    h2 = _rms(x1, gpre_ref[...]) * (1.0 + sc_ref[...]) + sh_ref[...]
    h2_ref[...] = h2.astype(BF16)


def _out_proj(osb, og, proj, rg_col, x, g_sb, g_gla, w, g_post, gt, g_pre, sc, sh, tm,
              rows_per_group):
    m, d = x.shape
    sbw, gw = osb.shape[1], og.shape[1]
    dh, dv = g_sb.shape[1], g_gla.shape[1]
    r = gt.shape[1]
    mod_spec = pl.BlockSpec((None, r, d), lambda i: ((i * tm) // rows_per_group, 0, 0))
    row = lambda width: pl.BlockSpec((1, width), lambda i: (0, 0))
    kernel = functools.partial(_out_proj_kernel, dh=dh, dv=dv)
    return pl.pallas_call(
        kernel,
        grid=(m // tm,),
        in_specs=[pl.BlockSpec((tm, sbw), lambda i: (i, 0)),
                  pl.BlockSpec((tm, gw), lambda i: (i, 0)),
                  pl.BlockSpec((tm, gw), lambda i: (i, rg_col // gw)),
                  pl.BlockSpec((tm, d), lambda i: (i, 0)),
                  row(dh), row(dv),
                  pl.BlockSpec((sbw + gw, d), lambda i: (0, 0)),
                  row(d), mod_spec, row(d), mod_spec, mod_spec],
        out_specs=[pl.BlockSpec((tm, d), lambda i: (i, 0)),
                   pl.BlockSpec((tm, d), lambda i: (i, 0))],
        out_shape=[jax.ShapeDtypeStruct((m, d), F32),
                   jax.ShapeDtypeStruct((m, d), BF16)],
        scratch_shapes=[pltpu.VMEM((tm, sbw + gw), BF16)],
        compiler_params=_params("arbitrary"),
        name="out_proj",
    )(osb, og, proj, x, g_sb, g_gla, w, g_post, gt, g_pre, sc, sh)


def _ffn_kernel(h_ref, wg_ref, wu_ref, wd_ref, x1_ref, gpost_ref, gt_ref, y_ref, acc_scr):
    j = pl.program_id(1)

    @pl.when(j == 0)
    def _():
        acc_scr[...] = jnp.zeros_like(acc_scr)

    h = h_ref[...]
    t = _silu(_dot(h, wg_ref[...])) * _dot(h, wu_ref[...])
    acc_scr[...] += _dot(t.astype(BF16), wd_ref[...])

    @pl.when(j == pl.num_programs(1) - 1)
    def _():
        y_ref[...] = x1_ref[...] + gt_ref[...] * _rms(acc_scr[...], gpost_ref[...])


def _ffn(h2, wg, wu, wd, x1, g_post, gt, tm, rows_per_group, tf=512):
    m, d = x1.shape
    f = wg.shape[1]
    r = gt.shape[1]
    return pl.pallas_call(
        _ffn_kernel,
        grid=(m // tm, f // tf),
        in_specs=[pl.BlockSpec((tm, d), lambda i, j: (i, 0)),
                  pl.BlockSpec((d, tf), lambda i, j: (0, j)),
                  pl.BlockSpec((d, tf), lambda i, j: (0, j)),
                  pl.BlockSpec((tf, d), lambda i, j: (j, 0)),
                  pl.BlockSpec((tm, d), lambda i, j: (i, 0)),
                  pl.BlockSpec((1, d), lambda i, j: (0, 0)),
                  pl.BlockSpec((None, r, d), lambda i, j: ((i * tm) // rows_per_group, 0, 0))],
        out_specs=pl.BlockSpec((tm, d), lambda i, j: (i, 0)),
        out_shape=jax.ShapeDtypeStruct((m, d), F32),
        scratch_shapes=[pltpu.VMEM((tm, d), F32)],
        compiler_params=_params("arbitrary", "arbitrary"),
        name="ffn",
    )(h2, wg, wu, wd, x1, g_post, gt)


def _suffix_sum_matrix():
    j = lax.broadcasted_iota(jnp.int32, (2 * KEY_BLOCK, 2 * KEY_BLOCK), 0) % KEY_BLOCK
    s = lax.broadcasted_iota(jnp.int32, (2 * KEY_BLOCK, 2 * KEY_BLOCK), 1)
    return jnp.where((j > s) | (s >= KEY_BLOCK), 1.0, 0.0).astype(BF16)


def kernel(x_prompt, x_sample, c_prompt, c_sample, cache_k, cache_v, page_table, state_gla, w_ada, b_ada, g_pre_mix, w_in, b_sb, w_gate_up, b_gate, g_sb_out, g_gla_out, w_out, g_post_mix, g_pre_ffn, w_ffn_gate, w_ffn_up, w_ffn_down, g_post_ffn):
    batch, seq, d = x_prompt.shape
    nb = x_sample.shape[0]
    depth, n_pool, page, sb_heads, dh = cache_k.shape
    gla_heads, dk, dv = state_gla.shape[2:]
    rank = w_gate_up.shape[1]
    assert depth == 1 and x_sample.shape[1] == 1
    sbw, kw, vw = sb_heads * dh, gla_heads * dk, gla_heads * dv
    assert 2 * kw == sbw and vw == sbw
    q_col, k_col, v_col, rg_col = 0, kw, sbw, 2 * sbw
    main = 3 * sbw + 2 * kw + 2 * vw

    row = lambda a: a.reshape(1, -1)
    uo = _suffix_sum_matrix()
    ones = jnp.ones((LANES, LANES), BF16)

    w_in0 = w_in[0]
    w_main = w_in0.astype(BF16)
    w_low = jnp.pad(w_in0[:, main:], ((0, 0), (0, LANES - rank))).astype(BF16)
    wgu = jnp.pad(w_gate_up[0], ((0, LANES - rank), (0, 0))).astype(BF16)
    w_o = w_out[0].astype(BF16)
    w_fg, w_fu, w_fd = (w[0].astype(BF16) for w in (w_ffn_gate, w_ffn_up, w_ffn_down))

    n_c = batch + nb
    c_all = jnp.pad(jnp.concatenate([c_prompt, c_sample], axis=0), ((0, -n_c % SUBLANES), (0, 0)))
    mod = _ada(c_all, w_ada[0], row(b_ada[0]))
    mod_p = [a.reshape(batch, 1, d) for a in jnp.split(mod[:batch], 6, axis=-1)]
    mod_s = [a.reshape(1, nb, d) for a in jnp.split(mod[batch:n_c], 6, axis=-1)]

    gains = dict(g_sb=row(g_sb_out[0]), g_gla=row(g_gla_out[0]), g_post=row(g_post_mix[0]),
                 g_pre=row(g_pre_ffn[0]))

    def rest_of_layer(x2, osb, og, proj, mods, tm, rows_per_group, ffn_tm):
        sh1, sc1, gt1, sh2, sc2, gt2 = mods
        x1, h2 = _out_proj(osb, og, proj, rg_col, x2, gains["g_sb"], gains["g_gla"], w_o,
                           gains["g_post"], gt1, gains["g_pre"], sc2, sh2, tm, rows_per_group)
        return _ffn(h2, w_fg, w_fu, w_fd, x1, row(g_post_ffn[0]), gt2, ffn_tm, rows_per_group)

    xp = x_prompt.reshape(batch * seq, d)
    qkv_p, k_p, v_p, gla_p, glow_p = _in_proj(xp, row(g_pre_mix[0]), mod_p[1], mod_p[0], w_main, w_low,
                                          512, seq, sb_heads, dh)
    osb_p = _sb_prompt(qkv_p, b_sb[0], uo, batch, seq, sb_heads, dh, tq=256)
    og_p, st_p = _gla_prompt(gla_p, glow_p, wgu, row(b_gate[0]), ones, batch, seq, gla_heads,
                             dk, dv, q_col, k_col, v_col)
    y_p = rest_of_layer(xp, osb_p, og_p, gla_p, mod_p, 256, seq, 512)

    xs = x_sample.reshape(nb, d)
    qkv_s, k_s, v_s, gla_s, glow_s = _in_proj(xs, row(g_pre_mix[0]), mod_s[1], mod_s[0], w_main, w_low,
                                          nb, nb, sb_heads, dh)
    bias_rows = jnp.broadcast_to(b_sb[0][:, None], (sb_heads, KEY_BLOCK))
    osb_s = _sb_decode(qkv_s[0].reshape(nb, sb_heads, dh),
                       cache_k[0].reshape(n_pool, page * sb_heads, dh),
                       cache_v[0].reshape(n_pool, page * sb_heads, dh),
                       page_table, bias_rows, uo)
    og_s, st_s = _gla_step(gla_s, glow_s, wgu, row(b_gate[0]), state_gla[0], dk, dv,
                           q_col, k_col, v_col)
    y_s = rest_of_layer(xs, osb_s.reshape(nb, sbw), og_s, gla_s, mod_s, nb, nb, nb)

    kv = lambda a, n: a.reshape(1, n, -1, sb_heads, dh)
    return (y_p.reshape(batch, seq, d), y_s.reshape(nb, 1, d),
            kv(k_p, batch), kv(v_p, batch),
            jnp.swapaxes(st_p, -1, -2)[None],
            kv(k_s, nb), kv(v_s, nb), st_s[None])
```

```python
import functools

import jax
import jax.numpy as jnp
from jax import lax
from jax.experimental import pallas as pl
from jax.experimental.pallas import tpu as pltpu

F32 = jnp.float32
BF16 = jnp.bfloat16

LANES = 128
SUBLANES = 8
VMEM_LIMIT_BYTES = 52 * 1024 * 1024

EPS = 1e-6
GLA_GATE_NORM = 16.0
GLA_CHUNK = 64
GLA_SUB = 8
KEY_BLOCK = 128


def _params(*sem):
    return pltpu.CompilerParams(dimension_semantics=sem, vmem_limit_bytes=VMEM_LIMIT_BYTES)


def _log_sigmoid(z):
    return jnp.minimum(z, 0.0) - jnp.log(1.0 + jnp.exp(-jnp.abs(z)))


def _silu(x):
    return x * (1.0 / (1.0 + jnp.exp(-x)))


def _rms(x, g):
    ms = jnp.mean(x * x, axis=-1, keepdims=True)
    return x * lax.rsqrt(ms + EPS) * g


def _split_bf16(x, pieces):
    out = []
    for _ in range(pieces - 1):
        p = x.astype(BF16)
        out.append(p)
        x = x - p.astype(F32)
    out.append(x.astype(BF16))
    return out


def _dot(a, b):
    return jnp.dot(a, b, preferred_element_type=F32)


def _dot_nt(a, b):
    return lax.dot_general(a, b, (((1,), (1,)), ((), ())), preferred_element_type=F32)


def _dot_tn(a, b):
    return lax.dot_general(a, b, (((0,), (0,)), ((), ())), preferred_element_type=F32)


def _ada_kernel(c_ref, w_ref, b_ref, o_ref):
    a = _silu(c_ref[...]).astype(BF16)
    o_ref[...] = _dot(a, w_ref[...].astype(BF16)) + b_ref[...]


def _ada(c, w, b, tn=1024):
    m, d = c.shape
    n = w.shape[1]
    return pl.pallas_call(
        _ada_kernel,
        grid=(n // tn,),
        in_specs=[pl.BlockSpec((m, d), lambda j: (0, 0)),
                  pl.BlockSpec((d, tn), lambda j: (0, j)),
                  pl.BlockSpec((1, tn), lambda j: (0, j))],
        out_specs=pl.BlockSpec((m, tn), lambda j: (0, j)),
        out_shape=jax.ShapeDtypeStruct((m, n), F32),
        compiler_params=_params("arbitrary"),
        name="ada_mod",
    )(c, w, b)


def _in_proj_kernel(x_ref, g_ref, sc_ref, sh_ref, w_ref, wl_ref, qkv_ref, krow_ref, vrow_ref,
                    gla_ref, ol_ref, h_scr, *, heads, dh, q_scale):
    j = pl.program_id(1)

    @pl.when(j == 0)
    def _():
        h = _rms(x_ref[...], g_ref[...]) * (1.0 + sc_ref[...]) + sh_ref[...]
        h_scr[...] = h.astype(BF16)
        ol_ref[...] = _dot(h_scr[...], wl_ref[...])

    res = _dot(h_scr[...], w_ref[...])

    @pl.when(j == 0)
    def _():
        qkv_ref[...] = (res * q_scale).astype(BF16)

    for tile, rows_ref in ((1, krow_ref), (2, vrow_ref)):
        @pl.when(j == tile)
        def _(rows_ref=rows_ref):
            qkv_ref[...] = res.astype(BF16)
            for h in range(heads):
                rows_ref[pl.ds(h, res.shape[0], stride=heads), :] = res[:, h * dh:(h + 1) * dh]

    @pl.when(j >= 3)
    def _():
        gla_ref[...] = res


def _in_proj(x, g, sc, sh, w, wl, tm, rows_per_group, heads, dh):
    m, d = x.shape
    tn = heads * dh
    r = sc.shape[1]
    mod_spec = pl.BlockSpec((None, r, d), lambda i, j: ((i * tm) // rows_per_group, 0, 0))
    kernel = functools.partial(_in_proj_kernel, heads=heads, dh=dh, q_scale=dh ** -0.5)
    return pl.pallas_call(
        kernel,
        grid=(m // tm, 6),
        in_specs=[pl.BlockSpec((tm, d), lambda i, j: (i, 0)),
                  pl.BlockSpec((1, d), lambda i, j: (0, 0)),
                  mod_spec, mod_spec,
                  pl.BlockSpec((d, tn), lambda i, j: (0, j)),
                  pl.BlockSpec((d, LANES), lambda i, j: (0, 0))],
        out_specs=[pl.BlockSpec((None, tm, tn), lambda i, j: (jnp.minimum(j, 2), i, 0)),
                   pl.BlockSpec((tm * heads, dh), lambda i, j: (i, 0)),
                   pl.BlockSpec((tm * heads, dh), lambda i, j: (i, 0)),
                   pl.BlockSpec((tm, tn), lambda i, j: (i, jnp.clip(j - 3, 0, 2))),
                   pl.BlockSpec((tm, LANES), lambda i, j: (i, 0))],
        out_shape=[jax.ShapeDtypeStruct((3, m, tn), BF16),
                   jax.ShapeDtypeStruct((m * heads, dh), F32),
                   jax.ShapeDtypeStruct((m * heads, dh), F32),
                   jax.ShapeDtypeStruct((m, 3 * tn), F32),
                   jax.ShapeDtypeStruct((m, LANES), F32)],
        scratch_shapes=[pltpu.VMEM((tm, d), BF16)],
        compiler_params=_params("arbitrary", "arbitrary"),
        name="in_proj",
    )(x, g, sc, sh, w, wl)


def _sb_group(z, carry, uo, mask):
    n = z.shape[1] // KEY_BLOCK
    lb = _log_sigmoid(z)
    lk = lb - z
    if mask is not None:
        lk = jnp.where(mask, lk, 0.0)
    hi, lo = _split_bf16(lk, 2)
    tails = [None] * n
    for s in reversed(range(n)):
        sl = slice(s * KEY_BLOCK, (s + 1) * KEY_BLOCK)
        ts = _dot(jnp.concatenate([hi[:, sl], lo[:, sl]], axis=1), uo)
        tails[s] = ts[:, :KEY_BLOCK] + carry
        carry = carry + ts[:, KEY_BLOCK:]
    a = jnp.exp(lb + jnp.concatenate(tails, axis=1))
    if mask is not None:
        a = jnp.where(mask, a, 0.0)
    return a.astype(BF16), carry


def _sb_decode_step(q_ref, bias_ref, uo_ref, gather_ref, spread_ref, k_refs, v_refs, carry_scr,
                    acc_scr, heads):
    pages = len(k_refs)
    q = q_ref[...]
    rows = k_refs[0].shape[0]
    own = (lax.broadcasted_iota(jnp.int32, (pages * heads, rows), 1) % heads
           == lax.broadcasted_iota(jnp.int32, (pages * heads, rows), 0) % heads)
    z_all = jnp.concatenate([_dot_nt(q, r[...].astype(BF16)) for r in k_refs], axis=0)
    pieces = _split_bf16(jnp.where(own, z_all, 0.0), 3)
    zr = _dot(jnp.concatenate(pieces, axis=1), gather_ref[...])
    zr = zr + jnp.concatenate([bias_ref[...]] * pages, axis=0)
    lb = _log_sigmoid(zr)
    hi, lo = _split_bf16(lb - zr, 2)
    ts = _dot(jnp.concatenate([hi, lo], axis=1), uo_ref[...])
    carry = carry_scr[...]
    tails = []
    for p in range(pages):
        tsp = ts[p * heads:(p + 1) * heads]
        tails.append(tsp[:, :KEY_BLOCK] + carry)
        carry = carry + tsp[:, KEY_BLOCK:]
    carry_scr[...] = carry
    a = jnp.exp(lb + jnp.concatenate(tails, axis=0)).astype(BF16)
    a_rows = jnp.where(own, _dot(a, spread_ref[...]), 0.0)
    acc = acc_scr[...]
    for p in range(pages):
        acc = acc + _dot(a_rows[p * heads:(p + 1) * heads].astype(BF16), v_refs[p][...].astype(BF16))
    acc_scr[...] = acc


def _sb_attn_kernel(pt_ref, q_ref, k_ref, v_ref, bias_ref, uo_ref, qs_ref, bias_rows_ref,
                    gather_ref, spread_ref, *refs, tq, group, pages, heads):
    k_refs, v_refs = refs[:pages], refs[pages:2 * pages]
    o_ref, os_ref, carry_scr, acc_scr, dcarry_scr, dacc_scr = refs[2 * pages:]
    qi = pl.program_id(2)
    h = pl.program_id(1)
    q = q_ref[...]
    bias = bias_ref[h]
    uo = uo_ref[...]
    n_diag = tq // KEY_BLOCK

    @pl.when(qi == 0)
    def _():
        dcarry_scr[...] = jnp.zeros_like(dcarry_scr)
        dacc_scr[...] = jnp.zeros_like(dacc_scr)

    def visit(first_block, n, newest):
        rows = n * KEY_BLOCK
        start = pl.multiple_of(first_block * KEY_BLOCK, KEY_BLOCK)
        kb = k_ref[pl.ds(start, rows), :]
        vb = v_ref[pl.ds(start, rows), :]
        mask = None
        if newest:
            mask = (lax.broadcasted_iota(jnp.int32, (tq, rows), 1) + (start - qi * tq)
                    < lax.broadcasted_iota(jnp.int32, (tq, rows), 0))
            _sb_decode_step(qs_ref, bias_rows_ref, uo_ref, gather_ref, spread_ref, k_refs, v_refs,
                            dcarry_scr, dacc_scr, heads)
        a, carry = _sb_group(_dot_nt(q, kb) + bias, carry_scr[...], uo, mask)
        carry_scr[...] = carry
        acc_scr[...] += _dot(a, vb)

    carry_scr[...] = jnp.zeros_like(carry_scr)
    acc_scr[...] = jnp.zeros_like(acc_scr)
    n_tot = (qi + 1) * n_diag
    n_groups = n_tot // group
    leftover = lax.rem(n_tot, group)

    @pl.when(n_groups > 0)
    def _():
        visit(n_tot - group, group, True)

    def body(g, _):
        visit(n_tot - (g + 1) * group, group, False)
        return 0

    lax.fori_loop(1, n_groups, body, 0)
    size = group // 2
    while size >= n_diag:
        take = lax.rem(lax.div(leftover, size), 2) == 1
        is_first = jnp.logical_and(n_groups == 0, lax.div(leftover, 2 * size) == 0)
        for newest in (True, False):
            pl.when(jnp.logical_and(take, is_first == newest))(
                functools.partial(visit, lax.rem(leftover, size), size, newest))
        size //= 2
    o_ref[...] = acc_scr[...]

    @pl.when(qi == pl.num_programs(2) - 1)
    def _():
        os_ref[...] = dacc_scr[...]


def _sb_attention(qkv, b_sb, uo, q_s, cache_k, cache_v, page_table, bias_rows, batch, seq, heads,
                  dh, tq, group=8):
    m = qkv.shape[1]
    nq = seq // tq
    nb, n_pages = page_table.shape
    rows = cache_k.shape[1]
    assert nb == batch * heads and n_pages % nq == 0 and rows == KEY_BLOCK * heads
    pages = n_pages // nq
    kernel = functools.partial(_sb_attn_kernel, tq=tq, group=group, pages=pages, heads=heads)
    key_of_row = lax.broadcasted_iota(jnp.int32, (rows, KEY_BLOCK), 0) // heads
    gather1 = jnp.where(key_of_row == lax.broadcasted_iota(jnp.int32, (rows, KEY_BLOCK), 1), 1.0, 0.0)
    gather = jnp.concatenate([gather1] * 3, axis=0).astype(BF16)
    spread = gather1.T.astype(BF16)

    def page_spec(j):
        return pl.BlockSpec(
            (None, rows, dh),
            lambda b, h, i, pt, j=j: (pt[b * heads + h, n_pages - 1 - (i * pages + j)], 0, 0))

    const = lambda a: pl.BlockSpec(a.shape, lambda b, h, i, pt: (0,) * a.ndim)
    grid_spec = pltpu.PrefetchScalarGridSpec(
        num_scalar_prefetch=1,
        grid=(batch, heads, nq),
        in_specs=[pl.BlockSpec((None, tq, dh), lambda b, h, i, pt: (0, b * nq + i, h)),
                  pl.BlockSpec((None, seq, dh), lambda b, h, i, pt: (1, b, h)),
                  pl.BlockSpec((None, seq, dh), lambda b, h, i, pt: (2, b, h)),
                  pl.BlockSpec(memory_space=pltpu.SMEM),
                  const(uo),
                  pl.BlockSpec((None, heads, dh), lambda b, h, i, pt: (b * heads + h, 0, 0)),
                  const(bias_rows), const(gather), const(spread)]
                 + [page_spec(j) for j in range(pages)] * 2,
        out_specs=[pl.BlockSpec((tq, dh), lambda b, h, i, pt: (b * nq + i, h)),
                   pl.BlockSpec((None, heads, dh), lambda b, h, i, pt: (b * heads + h, 0, 0))],
        scratch_shapes=[pltpu.VMEM((tq, KEY_BLOCK), F32), pltpu.VMEM((tq, dh), F32),
                        pltpu.VMEM((heads, KEY_BLOCK), F32), pltpu.VMEM((heads, dh), F32)],
    )
    return pl.pallas_call(
        kernel,
        grid_spec=grid_spec,
        out_shape=[jax.ShapeDtypeStruct((m, heads * dh), F32),
                   jax.ShapeDtypeStruct((nb, heads, dh), F32)],
        compiler_params=_params("arbitrary", "arbitrary", "arbitrary"),
        name="sb_attention",
    )(page_table, qkv, qkv, qkv, b_sb, uo, q_s, bias_rows, gather, spread,
      *([cache_k] * pages), *([cache_v] * pages))


def _gla_prefix_matrix():
    t = lax.broadcasted_iota(jnp.int32, (GLA_CHUNK, GLA_CHUNK), 0)
    s = lax.broadcasted_iota(jnp.int32, (GLA_CHUNK, GLA_CHUNK), 1)
    return jnp.where(s <= t, 1.0, 0.0).astype(BF16)


def _gla_intra(qs, ks, bs, ones_bf16):
    n = len(qs)
    c = GLA_CHUNK
    t_i = lax.broadcasted_iota(jnp.int32, (c, c), 0)
    s_i = lax.broadcasted_iota(jnp.int32, (c, c), 1)
    atts = [jnp.zeros((c, c), F32)] * n
    h = c // 2
    while h >= GLA_SUB:
        th, sh = t_i // h, s_i // h
        pair = ((th - sh - 1) | ((th & 1) ^ 1)) == 0
        for ci in range(n):
            q, k, b = qs[ci], ks[ci], bs[ci]
            ref = jnp.concatenate(
                [jnp.broadcast_to(b[j + h - 1:j + h], (2 * h, b.shape[1])) for j in range(0, c, 2 * h)],
                axis=0)
            ql = (q * jnp.exp(jnp.minimum(b - ref, 0.0))).astype(BF16)
            kl = (k * jnp.exp(jnp.minimum(ref - b, 0.0))).astype(BF16)
            atts[ci] = atts[ci] + jnp.where(pair, _dot_nt(ql, kl), 0.0)
        h //= 2
    prods = []
    for ci in range(n):
        q, k, b = qs[ci], ks[ci], bs[ci]
        for lo_ in range(0, c, GLA_SUB):
            qi, ki, bi = (a[lo_:lo_ + GLA_SUB] for a in (q, k, b))
            prods += [qi * ki[s:s + 1] * jnp.exp(jnp.minimum(bi - bi[s:s + 1], 0.0))
                      for s in range(GLA_SUB)]
    rsum = _dot(jnp.concatenate(prods, axis=0).astype(BF16), ones_bf16)
    lane = lax.broadcasted_iota(jnp.int32, (GLA_SUB, LANES), 1)
    trow = lax.broadcasted_iota(jnp.int32, (GLA_SUB, LANES), 0)
    for ci in range(n):
        strips = []
        for i, lo_ in enumerate(range(0, c, GLA_SUB)):
            base = (ci * (c // GLA_SUB) + i) * GLA_SUB * GLA_SUB
            strip = jnp.zeros((GLA_SUB, LANES), F32)
            for s in range(GLA_SUB):
                strip = jnp.where(lane == lo_ + s,
                                  rsum[base + s * GLA_SUB:base + (s + 1) * GLA_SUB], strip)
            strips.append(jnp.where(lane - lo_ <= trow, strip, 0.0)[:, :c])
        atts[ci] = atts[ci] + jnp.concatenate(strips, axis=0)
    return atts


def _gla_prompt_kernel(q_ref, k_ref, v_ref, gl_ref, wg_ref, bg_ref, ones_ref, ll_ref, o_ref,
                       st_ref, st_scr, *, n_inner, scale):
    @pl.when(pl.program_id(2) == 0)
    def _():
        st_scr[...] = jnp.zeros_like(st_scr)

    c = GLA_CHUNK
    ll = ll_ref[...]
    g = _log_sigmoid(_dot(gl_ref[...].astype(BF16), wg_ref[...]) + bg_ref[...]) * (1.0 / GLA_GATE_NORM)
    chunks = [slice(ci * c, (ci + 1) * c) for ci in range(n_inner)]
    bs = [sum(_dot(ll, piece) for piece in _split_bf16(g[sl], 3)) for sl in chunks]
    qs = [q_ref[sl, :] * scale for sl in chunks]
    ks = [k_ref[sl, :] for sl in chunks]
    atts = _gla_intra(qs, ks, bs, ones_ref[...])
    st = st_scr[...]
    for ci, sl in enumerate(chunks):
        q, k, b = qs[ci], ks[ci], bs[ci]
        v16 = v_ref[sl, :].astype(BF16)
        o_ref[sl, :] = (_dot_nt((q * jnp.exp(b)).astype(BF16), st.astype(BF16))
                        + _dot(atts[ci].astype(BF16), v16))
        b_last = b[c - 1:c]
        kdec = (k * jnp.exp(b_last - b)).astype(BF16)
        st = st * jnp.exp(b_last) + _dot_tn(v16, kdec)
    st_scr[...] = st
    st_ref[...] = st


def _gla_prompt(proj, glow, wgu, b_gate, ones, batch, seq, heads, dk, dv, q_col, k_col, v_col,
                n_inner=4):
    m = proj.shape[0]
    cb = n_inner * GLA_CHUNK
    nc = seq // cb
    kernel = functools.partial(_gla_prompt_kernel, n_inner=n_inner, scale=dk ** -0.5)
    return pl.pallas_call(
        kernel,
        grid=(batch, heads, nc),
        in_specs=[pl.BlockSpec((cb, dk), lambda b, h, c: (b * nc + c, q_col // dk + h)),
                  pl.BlockSpec((cb, dk), lambda b, h, c: (b * nc + c, k_col // dk + h)),
                  pl.BlockSpec((cb, dv), lambda b, h, c: (b * nc + c, v_col // dv + h)),
                  pl.BlockSpec((cb, LANES), lambda b, h, c: (b * nc + c, 0)),
                  pl.BlockSpec((LANES, dk), lambda b, h, c: (0, h)),
                  pl.BlockSpec((1, dk), lambda b, h, c: (0, h)),
                  pl.BlockSpec((LANES, LANES), lambda b, h, c: (0, 0)),
                  pl.BlockSpec((GLA_CHUNK, GLA_CHUNK), lambda b, h, c: (0, 0))],
        out_specs=[pl.BlockSpec((cb, dv), lambda b, h, c: (b * nc + c, h)),
                   pl.BlockSpec((None, None, dv, dk), lambda b, h, c: (b, h, 0, 0))],
        out_shape=[jax.ShapeDtypeStruct((m, heads * dv), F32),
                   jax.ShapeDtypeStruct((batch, heads, dv, dk), F32)],
        scratch_shapes=[pltpu.VMEM((dv, dk), F32)],
        compiler_params=_params("arbitrary", "arbitrary", "arbitrary"),
        name="gla_prompt",
    )(proj, proj, proj, glow, wgu, b_gate, ones, _gla_prefix_matrix())


def _gla_step_kernel(q_ref, k_ref, v_ref, gl_ref, wg_ref, bg_ref, s_ref, o_ref, so_ref, *,
                     heads, dk, dv, scale):
    g = _log_sigmoid(_dot(gl_ref[...].astype(BF16), wg_ref[...]) + bg_ref[...]) * (1.0 / GLA_GATE_NORM)
    q = q_ref[...] * scale
    k = k_ref[...]
    v = v_ref[...]
    eye = (lax.broadcasted_iota(jnp.int32, (dk, dk), 0)
           == lax.broadcasted_iota(jnp.int32, (dk, dk), 1))

    def column(x_row):
        return jnp.sum(jnp.where(eye, x_row, 0.0), axis=1, keepdims=True)

    for h in range(heads):
        ksl = slice(h * dk, (h + 1) * dk)
        vsl = slice(h * dv, (h + 1) * dv)
        s_new = column(jnp.exp(g[:, ksl])) * s_ref[h] + column(k[:, ksl]) * v[:, vsl]
        so_ref[h] = s_new
        o_ref[:, vsl] = jnp.sum(column(q[:, ksl]) * s_new, axis=0, keepdims=True)


def _gla_step(proj, glow, wgu, b_gate, state, dk, dv, q_col, k_col, v_col):
    nb, heads = state.shape[0], state.shape[1]
    kw, vw = heads * dk, heads * dv
    kernel = functools.partial(_gla_step_kernel, heads=heads, dk=dk, dv=dv, scale=dk ** -0.5)
    proj3 = proj.reshape(nb, 1, proj.shape[1])
    glow3 = glow.reshape(nb, 1, LANES)
    o, s_new = pl.pallas_call(
        kernel,
        grid=(nb,),
        in_specs=[pl.BlockSpec((None, 1, kw), lambda b: (b, 0, q_col // kw)),
                  pl.BlockSpec((None, 1, kw), lambda b: (b, 0, k_col // kw)),
                  pl.BlockSpec((None, 1, vw), lambda b: (b, 0, v_col // vw)),
                  pl.BlockSpec((None, 1, LANES), lambda b: (b, 0, 0)),
                  pl.BlockSpec((LANES, kw), lambda b: (0, 0)),
                  pl.BlockSpec((1, kw), lambda b: (0, 0)),
                  pl.BlockSpec((None, heads, dk, dv), lambda b: (b, 0, 0, 0))],
        out_specs=[pl.BlockSpec((None, 1, vw), lambda b: (b, 0, 0)),
                   pl.BlockSpec((None, heads, dk, dv), lambda b: (b, 0, 0, 0))],
        out_shape=[jax.ShapeDtypeStruct((nb, 1, vw), F32),
                   jax.ShapeDtypeStruct(state.shape, F32)],
        compiler_params=_params("arbitrary"),
        name="gla_step",
    )(proj3, proj3, proj3, glow3, wgu, b_gate, state)
    return o.reshape(nb, vw), s_new


def _out_proj_kernel(osb_ref, og_ref, rg_ref, x_ref, gsb_ref, ggla_ref, w_ref, gpost_ref,
                     gt_ref, gpre_ref, sc_ref, sh_ref, x1_ref, h2_ref, mix_scr, *, dh, dv):
    sbw = osb_ref.shape[1]
    for h in range(sbw // dh):
        sl = slice(h * dh, (h + 1) * dh)
        mix_scr[:, sl] = _rms(osb_ref[:, sl], gsb_ref[...]).astype(BF16)
    for h in range(og_ref.shape[1] // dv):
        sl = slice(h * dv, (h + 1) * dv)
        y = _rms(og_ref[:, sl], ggla_ref[...]) * _silu(rg_ref[:, sl])
        mix_scr[:, sbw + h * dv:sbw + (h + 1) * dv] = y.astype(BF16)
    m = _dot(mix_scr[...], w_ref[...])
    x1 = x_ref[...] + gt_ref[...] * _rms(m, gpost_ref[...])
    x1_ref[...] = x1
    h2 = _rms(x1, gpre_ref[...]) * (1.0 + sc_ref[...]) + sh_ref[...]
    h2_ref[...] = h2.astype(BF16)


def _out_proj(osb, og, proj, rg_col, x, g_sb, g_gla, w, g_post, gt, g_pre, sc, sh, tm,
              rows_per_group):
    m, d = x.shape
    sbw, gw = osb.shape[1], og.shape[1]
    dh, dv = g_sb.shape[1], g_gla.shape[1]
    r = gt.shape[1]
    mod_spec = pl.BlockSpec((None, r, d), lambda i: ((i * tm) // rows_per_group, 0, 0))
    row = lambda width: pl.BlockSpec((1, width), lambda i: (0, 0))
    kernel = functools.partial(_out_proj_kernel, dh=dh, dv=dv)
    return pl.pallas_call(
        kernel,
        grid=(m // tm,),
        in_specs=[pl.BlockSpec((tm, sbw), lambda i: (i, 0)),
                  pl.BlockSpec((tm, gw), lambda i: (i, 0)),
                  pl.BlockSpec((tm, gw), lambda i: (i, rg_col // gw)),
                  pl.BlockSpec((tm, d), lambda i: (i, 0)),
                  row(dh), row(dv),
                  pl.BlockSpec((sbw + gw, d), lambda i: (0, 0)),
                  row(d), mod_spec, row(d), mod_spec, mod_spec],
        out_specs=[pl.BlockSpec((tm, d), lambda i: (i, 0)),
                   pl.BlockSpec((tm, d), lambda i: (i, 0))],
        out_shape=[jax.ShapeDtypeStruct((m, d), F32),
                   jax.ShapeDtypeStruct((m, d), BF16)],
        scratch_shapes=[pltpu.VMEM((tm, sbw + gw), BF16)],
        compiler_params=_params("arbitrary"),
        name="out_proj",
    )(osb, og, proj, x, g_sb, g_gla, w, g_post, gt, g_pre, sc, sh)


def _ffn_kernel(h_ref, wg_ref, wu_ref, wd_ref, x1_ref, gpost_ref, gt_ref, y_ref, acc_scr):
    j = pl.program_id(1)

    @pl.when(j == 0)
    def _():
        acc_scr[...] = jnp.zeros_like(acc_scr)

    h = h_ref[...]
    t = _silu(_dot(h, wg_ref[...])) * _dot(h, wu_ref[...])
    acc_scr[...] += _dot(t.astype(BF16), wd_ref[...])

    @pl.when(j == pl.num_programs(1) - 1)
    def _():
        y_ref[...] = x1_ref[...] + gt_ref[...] * _rms(acc_scr[...], gpost_ref[...])


def _ffn(h2, wg, wu, wd, x1, g_post, gt, tm, rows_per_group, tf=512):
    m, d = x1.shape
    f = wg.shape[1]
    r = gt.shape[1]
    return pl.pallas_call(
        _ffn_kernel,
        grid=(m // tm, f // tf),
        in_specs=[pl.BlockSpec((tm, d), lambda i, j: (i, 0)),
                  pl.BlockSpec((d, tf), lambda i, j: (0, j)),
                  pl.BlockSpec((d, tf), lambda i, j: (0, j)),
                  pl.BlockSpec((tf, d), lambda i, j: (j, 0)),
                  pl.BlockSpec((tm, d), lambda i, j: (i, 0)),
                  pl.BlockSpec((1, d), lambda i, j: (0, 0)),
                  pl.BlockSpec((None, r, d), lambda i, j: ((i * tm) // rows_per_group, 0, 0))],
        out_specs=pl.BlockSpec((tm, d), lambda i, j: (i, 0)),
        out_shape=jax.ShapeDtypeStruct((m, d), F32),
        scratch_shapes=[pltpu.VMEM((tm, d), F32)],
        compiler_params=_params("arbitrary", "arbitrary"),
        name="ffn",
    )(h2, wg, wu, wd, x1, g_post, gt)


def _suffix_sum_matrix():
    j = lax.broadcasted_iota(jnp.int32, (2 * KEY_BLOCK, 2 * KEY_BLOCK), 0) % KEY_BLOCK
    s = lax.broadcasted_iota(jnp.int32, (2 * KEY_BLOCK, 2 * KEY_BLOCK), 1)
    return jnp.where((j > s) | (s >= KEY_BLOCK), 1.0, 0.0).astype(BF16)


def kernel(x_prompt, x_sample, c_prompt, c_sample, cache_k, cache_v, page_table, state_gla, w_ada, b_ada, g_pre_mix, w_in, b_sb, w_gate_up, b_gate, g_sb_out, g_gla_out, w_out, g_post_mix, g_pre_ffn, w_ffn_gate, w_ffn_up, w_ffn_down, g_post_ffn):
    batch, seq, d = x_prompt.shape
    nb = x_sample.shape[0]
    depth, n_pool, page, sb_heads, dh = cache_k.shape
    gla_heads, dk, dv = state_gla.shape[2:]
    rank = w_gate_up.shape[1]
    assert depth == 1 and x_sample.shape[1] == 1
    sbw, kw, vw = sb_heads * dh, gla_heads * dk, gla_heads * dv
    assert 2 * kw == sbw and vw == sbw
    q_col, k_col, v_col, rg_col = 0, kw, sbw, 2 * sbw
    main = 3 * sbw + 2 * kw + 2 * vw

    row = lambda a: a.reshape(1, -1)
    uo = _suffix_sum_matrix()
    ones = jnp.ones((LANES, LANES), BF16)

    w_in0 = w_in[0]
    w_main = w_in0.astype(BF16)
    w_low = jnp.pad(w_in0[:, main:], ((0, 0), (0, LANES - rank))).astype(BF16)
    wgu = jnp.pad(w_gate_up[0], ((0, LANES - rank), (0, 0))).astype(BF16)
    w_o = w_out[0].astype(BF16)
    w_fg, w_fu, w_fd = (w[0].astype(BF16) for w in (w_ffn_gate, w_ffn_up, w_ffn_down))

    n_c = batch + nb
    c_all = jnp.pad(jnp.concatenate([c_prompt, c_sample], axis=0), ((0, -n_c % SUBLANES), (0, 0)))
    mod = _ada(c_all, w_ada[0], row(b_ada[0]))
    mod_p = [a.reshape(batch, 1, d) for a in jnp.split(mod[:batch], 6, axis=-1)]
    mod_s = [a.reshape(1, nb, d) for a in jnp.split(mod[batch:n_c], 6, axis=-1)]

    gains = dict(g_sb=row(g_sb_out[0]), g_gla=row(g_gla_out[0]), g_post=row(g_post_mix[0]),
                 g_pre=row(g_pre_ffn[0]))

    def rest_of_layer(x2, osb, og, proj, mods, tm, rows_per_group, ffn_tm):
        sh1, sc1, gt1, sh2, sc2, gt2 = mods
        x1, h2 = _out_proj(osb, og, proj, rg_col, x2, gains["g_sb"], gains["g_gla"], w_o,
                           gains["g_post"], gt1, gains["g_pre"], sc2, sh2, tm, rows_per_group)
        return _ffn(h2, w_fg, w_fu, w_fd, x1, row(g_post_ffn[0]), gt2, ffn_tm, rows_per_group)

    xp = x_prompt.reshape(batch * seq, d)
    xs = x_sample.reshape(nb, d)
    qkv_p, k_p, v_p, gla_p, glow_p = _in_proj(xp, row(g_pre_mix[0]), mod_p[1], mod_p[0], w_main, w_low,
                                              512, seq, sb_heads, dh)
    qkv_s, k_s, v_s, gla_s, glow_s = _in_proj(xs, row(g_pre_mix[0]), mod_s[1], mod_s[0], w_main, w_low,
                                              nb, nb, sb_heads, dh)

    bias_rows = jnp.broadcast_to(b_sb[0][:, None], (sb_heads, KEY_BLOCK))
    osb_p, osb_s = _sb_attention(qkv_p, b_sb[0], uo, qkv_s[0].reshape(nb, sb_heads, dh),
                                 cache_k[0].reshape(n_pool, page * sb_heads, dh),
                                 cache_v[0].reshape(n_pool, page * sb_heads, dh),
                                 page_table, bias_rows, batch, seq, sb_heads, dh, tq=256)

    og_p, st_p = _gla_prompt(gla_p, glow_p, wgu, row(b_gate[0]), ones, batch, seq, gla_heads,
                             dk, dv, q_col, k_col, v_col)
    y_p = rest_of_layer(xp, osb_p, og_p, gla_p, mod_p, 256, seq, 512)
    og_s, st_s = _gla_step(gla_s, glow_s, wgu, row(b_gate[0]), state_gla[0], dk, dv,
                           q_col, k_col, v_col)
    y_s = rest_of_layer(xs, osb_s.reshape(nb, sbw), og_s, gla_s, mod_s, nb, nb, nb)

    kv = lambda a, n: a.reshape(1, n, -1, sb_heads, dh)
    return (y_p.reshape(batch, seq, d), y_s.reshape(nb, 1, d),
            kv(k_p, batch), kv(v_p, batch),
            jnp.swapaxes(st_p, -1, -2)[None],
            kv(k_s, nb), kv(v_s, nb), st_s[None])
```

```python
import functools

import jax
import jax.numpy as jnp
from jax import lax
from jax.experimental import pallas as pl
from jax.experimental.pallas import tpu as pltpu

F32 = jnp.float32
BF16 = jnp.bfloat16

LANES = 128
SUBLANES = 8
VMEM_LIMIT_BYTES = 52 * 1024 * 1024

EPS = 1e-6
GLA_GATE_NORM = 16.0
GLA_CHUNK = 64
GLA_SUB = 8
KEY_BLOCK = 128


def _params(*sem):
    return pltpu.CompilerParams(dimension_semantics=sem, vmem_limit_bytes=VMEM_LIMIT_BYTES)


def _log_sigmoid(z):
    return jnp.minimum(z, 0.0) - jnp.log(1.0 + jnp.exp(-jnp.abs(z)))


def _silu(x):
    return x * (1.0 / (1.0 + jnp.exp(-x)))


def _rms(x, g):
    ms = jnp.mean(x * x, axis=-1, keepdims=True)
    return x * lax.rsqrt(ms + EPS) * g


def _split_bf16(x, pieces):
    out = []
    for _ in range(pieces - 1):
        p = x.astype(BF16)
        out.append(p)
        x = x - p.astype(F32)
    out.append(x.astype(BF16))
    return out


def _dot(a, b):
    return jnp.dot(a, b, preferred_element_type=F32)


def _dot_nt(a, b):
    return lax.dot_general(a, b, (((1,), (1,)), ((), ())), preferred_element_type=F32)


def _dot_tn(a, b):
    return lax.dot_general(a, b, (((0,), (0,)), ((), ())), preferred_element_type=F32)


def _ada_kernel(c_ref, w_ref, b_ref, o_ref):
    a = _silu(c_ref[...]).astype(BF16)
    o_ref[...] = _dot(a, w_ref[...].astype(BF16)) + b_ref[...]


def _ada(c, w, b, tn=1024):
    m, d = c.shape
    n = w.shape[1]
    return pl.pallas_call(
        _ada_kernel,
        grid=(n // tn,),
        in_specs=[pl.BlockSpec((m, d), lambda j: (0, 0)),
                  pl.BlockSpec((d, tn), lambda j: (0, j)),
                  pl.BlockSpec((1, tn), lambda j: (0, j))],
        out_specs=pl.BlockSpec((m, tn), lambda j: (0, j)),
        out_shape=jax.ShapeDtypeStruct((m, n), F32),
        compiler_params=_params("arbitrary"),
        name="ada_mod",
    )(c, w, b)


def _in_proj_kernel(x_ref, g_ref, sc_ref, sh_ref, w_ref, wl_ref, qkv_ref, krow_ref, vrow_ref,
                    gla_ref, ol_ref, h_scr, *, heads, dh, q_scale):
    j = pl.program_id(1)

    @pl.when(j == 0)
    def _():
        h = _rms(x_ref[...], g_ref[...]) * (1.0 + sc_ref[...]) + sh_ref[...]
        h_scr[...] = h.astype(BF16)
        ol_ref[...] = _dot(h_scr[...], wl_ref[...])

    def product():
        return _dot(h_scr[...], w_ref[...])

    @pl.when(j == 0)
    def _():
        qkv_ref[...] = (product() * q_scale).astype(BF16)

    for tile, rows_ref in ((1, krow_ref), (2, vrow_ref)):
        @pl.when(j == tile)
        def _(rows_ref=rows_ref):
            res = product()
            qkv_ref[...] = res.astype(BF16)
            for h in range(heads):
                rows_ref[pl.ds(h, res.shape[0], stride=heads), :] = res[:, h * dh:(h + 1) * dh]

    @pl.when(j >= 3)
    def _():
        gla_ref[...] = product()


def _in_proj(x, g, sc, sh, w, wl, tm, rows_per_group, heads, dh):
    m, d = x.shape
    tn = heads * dh
    r = sc.shape[1]
    mod_spec = pl.BlockSpec((None, r, d), lambda i, j: ((i * tm) // rows_per_group, 0, 0))
    kernel = functools.partial(_in_proj_kernel, heads=heads, dh=dh, q_scale=dh ** -0.5)
    return pl.pallas_call(
        kernel,
        grid=(m // tm, 6),
        in_specs=[pl.BlockSpec((tm, d), lambda i, j: (i, 0)),
                  pl.BlockSpec((1, d), lambda i, j: (0, 0)),
                  mod_spec, mod_spec,
                  pl.BlockSpec((d, tn), lambda i, j: (0, j)),
                  pl.BlockSpec((d, LANES), lambda i, j: (0, 0))],
        out_specs=[pl.BlockSpec((None, tm, tn), lambda i, j: (jnp.minimum(j, 2), i, 0)),
                   pl.BlockSpec((tm * heads, dh), lambda i, j: (i, 0)),
                   pl.BlockSpec((tm * heads, dh), lambda i, j: (i, 0)),
                   pl.BlockSpec((tm, tn), lambda i, j: (i, jnp.clip(j - 3, 0, 2))),
                   pl.BlockSpec((tm, LANES), lambda i, j: (i, 0))],
        out_shape=[jax.ShapeDtypeStruct((3, m, tn), BF16),
                   jax.ShapeDtypeStruct((m * heads, dh), F32),
                   jax.ShapeDtypeStruct((m * heads, dh), F32),
                   jax.ShapeDtypeStruct((m, 3 * tn), F32),
                   jax.ShapeDtypeStruct((m, LANES), F32)],
        scratch_shapes=[pltpu.VMEM((tm, d), BF16)],
        compiler_params=_params("arbitrary", "arbitrary"),
        name="in_proj",
    )(x, g, sc, sh, w, wl)


def _sb_group(z, carry, uo, mask):
    n = z.shape[1] // KEY_BLOCK
    lb = _log_sigmoid(z)
    lk = lb - z
    if mask is not None:
        lk = jnp.where(mask, lk, 0.0)
    hi, lo = _split_bf16(lk, 2)
    tails = [None] * n
    for s in reversed(range(n)):
        sl = slice(s * KEY_BLOCK, (s + 1) * KEY_BLOCK)
        ts = _dot(jnp.concatenate([hi[:, sl], lo[:, sl]], axis=1), uo)
        tails[s] = ts[:, :KEY_BLOCK] + carry
        carry = carry + ts[:, KEY_BLOCK:]
    a = jnp.exp(lb + jnp.concatenate(tails, axis=1))
    if mask is not None:
        a = jnp.where(mask, a, 0.0)
    return a.astype(BF16), carry


def _sb_decode_step(q_ref, bias_ref, uo_ref, gather_ref, spread_ref, k_refs, v_refs, carry_scr,
                    acc_scr, heads):
    pages = len(k_refs)
    q = q_ref[...]
    rows = k_refs[0].shape[0]
    own = (lax.broadcasted_iota(jnp.int32, (pages * heads, rows), 1) % heads
           == lax.broadcasted_iota(jnp.int32, (pages * heads, rows), 0) % heads)
    z_all = jnp.concatenate([_dot_nt(q, r[...].astype(BF16)) for r in k_refs], axis=0)
    pieces = _split_bf16(jnp.where(own, z_all, 0.0), 3)
    zr = _dot(jnp.concatenate(pieces, axis=1), gather_ref[...])
    zr = zr + jnp.concatenate([bias_ref[...]] * pages, axis=0)
    lb = _log_sigmoid(zr)
    hi, lo = _split_bf16(lb - zr, 2)
    ts = _dot(jnp.concatenate([hi, lo], axis=1), uo_ref[...])
    carry = carry_scr[...]
    tails = []
    for p in range(pages):
        tsp = ts[p * heads:(p + 1) * heads]
        tails.append(tsp[:, :KEY_BLOCK] + carry)
        carry = carry + tsp[:, KEY_BLOCK:]
    carry_scr[...] = carry
    a = jnp.exp(lb + jnp.concatenate(tails, axis=0)).astype(BF16)
    a_rows = jnp.where(own, _dot(a, spread_ref[...]), 0.0)
    acc = acc_scr[...]
    for p in range(pages):
        acc = acc + _dot(a_rows[p * heads:(p + 1) * heads].astype(BF16), v_refs[p][...].astype(BF16))
    acc_scr[...] = acc


def _sb_attn_kernel(pt_ref, q_ref, k_ref, v_ref, bias_ref, uo_ref, qs_ref, bias_rows_ref,
                    gather_ref, spread_ref, *refs, tq, group, pages, heads):
    k_refs, v_refs = refs[:pages], refs[pages:2 * pages]
    o_ref, os_ref, carry_scr, acc_scr, dcarry_scr, dacc_scr = refs[2 * pages:]
    qi = pl.program_id(2)
    h = pl.program_id(1)
    q = q_ref[...]
    bias = bias_ref[h]
    uo = uo_ref[...]
    n_diag = tq // KEY_BLOCK

    @pl.when(qi == 0)
    def _():
        dcarry_scr[...] = jnp.zeros_like(dcarry_scr)
        dacc_scr[...] = jnp.zeros_like(dacc_scr)

    def visit(first_block, n, newest):
        rows = n * KEY_BLOCK
        start = pl.multiple_of(first_block * KEY_BLOCK, KEY_BLOCK)
        kb = k_ref[pl.ds(start, rows), :]
        vb = v_ref[pl.ds(start, rows), :]
        mask = None
        if newest:
            mask = (lax.broadcasted_iota(jnp.int32, (tq, rows), 1) + (start - qi * tq)
                    < lax.broadcasted_iota(jnp.int32, (tq, rows), 0))
            _sb_decode_step(qs_ref, bias_rows_ref, uo_ref, gather_ref, spread_ref, k_refs, v_refs,
                            dcarry_scr, dacc_scr, heads)
        a, carry = _sb_group(_dot_nt(q, kb) + bias, carry_scr[...], uo, mask)
        carry_scr[...] = carry
        acc_scr[...] += _dot(a, vb)

    carry_scr[...] = jnp.zeros_like(carry_scr)
    acc_scr[...] = jnp.zeros_like(acc_scr)
    n_tot = (qi + 1) * n_diag
    n_groups = n_tot // group
    leftover = lax.rem(n_tot, group)

    @pl.when(n_groups > 0)
    def _():
        visit(n_tot - group, group, True)

    def body(g, _):
        visit(n_tot - (g + 1) * group, group, False)
        return 0

    lax.fori_loop(1, n_groups, body, 0)
    size = group // 2
    while size >= n_diag:
        take = lax.rem(lax.div(leftover, size), 2) == 1
        is_first = jnp.logical_and(n_groups == 0, lax.div(leftover, 2 * size) == 0)
        for newest in (True, False):
            pl.when(jnp.logical_and(take, is_first == newest))(
                functools.partial(visit, lax.rem(leftover, size), size, newest))
        size //= 2
    o_ref[...] = acc_scr[...]

    @pl.when(qi == pl.num_programs(2) - 1)
    def _():
        os_ref[...] = dacc_scr[...]


def _sb_attention(qkv, b_sb, uo, q_s, cache_k, cache_v, page_table, bias_rows, batch, seq, heads,
                  dh, tq, group=8):
    m = qkv.shape[1]
    nq = seq // tq
    nb, n_pages = page_table.shape
    rows = cache_k.shape[1]
    assert nb == batch * heads and n_pages % nq == 0 and rows == KEY_BLOCK * heads
    pages = n_pages // nq
    kernel = functools.partial(_sb_attn_kernel, tq=tq, group=group, pages=pages, heads=heads)
    key_of_row = lax.broadcasted_iota(jnp.int32, (rows, KEY_BLOCK), 0) // heads
    gather1 = jnp.where(key_of_row == lax.broadcasted_iota(jnp.int32, (rows, KEY_BLOCK), 1), 1.0, 0.0)
    gather = jnp.concatenate([gather1] * 3, axis=0).astype(BF16)
    spread = gather1.T.astype(BF16)

    def page_spec(j):
        return pl.BlockSpec(
            (None, rows, dh),
            lambda b, h, i, pt, j=j: (pt[b * heads + h, n_pages - 1 - (i * pages + j)], 0, 0))

    const = lambda a: pl.BlockSpec(a.shape, lambda b, h, i, pt: (0,) * a.ndim)
    grid_spec = pltpu.PrefetchScalarGridSpec(
        num_scalar_prefetch=1,
        grid=(batch, heads, nq),
        in_specs=[pl.BlockSpec((None, tq, dh), lambda b, h, i, pt: (0, b * nq + i, h)),
                  pl.BlockSpec((None, seq, dh), lambda b, h, i, pt: (1, b, h)),
                  pl.BlockSpec((None, seq, dh), lambda b, h, i, pt: (2, b, h)),
                  pl.BlockSpec(memory_space=pltpu.SMEM),
                  const(uo),
                  pl.BlockSpec((None, heads, dh), lambda b, h, i, pt: (b * heads + h, 0, 0)),
                  const(bias_rows), const(gather), const(spread)]
                 + [page_spec(j) for j in range(pages)] * 2,
        out_specs=[pl.BlockSpec((tq, dh), lambda b, h, i, pt: (b * nq + i, h)),
                   pl.BlockSpec((None, heads, dh), lambda b, h, i, pt: (b * heads + h, 0, 0))],
        scratch_shapes=[pltpu.VMEM((tq, KEY_BLOCK), F32), pltpu.VMEM((tq, dh), F32),
                        pltpu.VMEM((heads, KEY_BLOCK), F32), pltpu.VMEM((heads, dh), F32)],
    )
    return pl.pallas_call(
        kernel,
        grid_spec=grid_spec,
        out_shape=[jax.ShapeDtypeStruct((m, heads * dh), F32),
                   jax.ShapeDtypeStruct((nb, heads, dh), F32)],
        compiler_params=_params("arbitrary", "arbitrary", "arbitrary"),
        name="sb_attention",
    )(page_table, qkv, qkv, qkv, b_sb, uo, q_s, bias_rows, gather, spread,
      *([cache_k] * pages), *([cache_v] * pages))


def _gla_prefix_matrix():
    t = lax.broadcasted_iota(jnp.int32, (GLA_CHUNK, GLA_CHUNK), 0)
    s = lax.broadcasted_iota(jnp.int32, (GLA_CHUNK, GLA_CHUNK), 1)
    return jnp.where(s <= t, 1.0, 0.0).astype(BF16)


def _gla_intra(qs, ks, bs, ones_bf16):
    n = len(qs)
    c = GLA_CHUNK
    t_i = lax.broadcasted_iota(jnp.int32, (c, c), 0)
    s_i = lax.broadcasted_iota(jnp.int32, (c, c), 1)
    atts = [jnp.zeros((c, c), F32)] * n
    h = c // 2
    while h >= GLA_SUB:
        th, sh = t_i // h, s_i // h
        pair = ((th - sh - 1) | ((th & 1) ^ 1)) == 0
        for ci in range(n):
            q, k, b = qs[ci], ks[ci], bs[ci]
            ref = jnp.concatenate(
                [jnp.broadcast_to(b[j + h - 1:j + h], (2 * h, b.shape[1])) for j in range(0, c, 2 * h)],
                axis=0)
            ql = (q * jnp.exp(jnp.minimum(b - ref, 0.0))).astype(BF16)
            kl = (k * jnp.exp(jnp.minimum(ref - b, 0.0))).astype(BF16)
            atts[ci] = atts[ci] + jnp.where(pair, _dot_nt(ql, kl), 0.0)
        h //= 2
    prods = []
    for ci in range(n):
        q, k, b = qs[ci], ks[ci], bs[ci]
        for lo_ in range(0, c, GLA_SUB):
            qi, ki, bi = (a[lo_:lo_ + GLA_SUB] for a in (q, k, b))
            prods += [qi * ki[s:s + 1] * jnp.exp(jnp.minimum(bi - bi[s:s + 1], 0.0))
                      for s in range(GLA_SUB)]
    rsum = _dot(jnp.concatenate(prods, axis=0).astype(BF16), ones_bf16)
    lane = lax.broadcasted_iota(jnp.int32, (GLA_SUB, LANES), 1)
    trow = lax.broadcasted_iota(jnp.int32, (GLA_SUB, LANES), 0)
    for ci in range(n):
        strips = []
        for i, lo_ in enumerate(range(0, c, GLA_SUB)):
            base = (ci * (c // GLA_SUB) + i) * GLA_SUB * GLA_SUB
            strip = jnp.zeros((GLA_SUB, LANES), F32)
            for s in range(GLA_SUB):
                strip = jnp.where(lane == lo_ + s,
                                  rsum[base + s * GLA_SUB:base + (s + 1) * GLA_SUB], strip)
            strips.append(jnp.where(lane - lo_ <= trow, strip, 0.0)[:, :c])
        atts[ci] = atts[ci] + jnp.concatenate(strips, axis=0)
    return atts


def _gla_prompt_kernel(q_ref, k_ref, v_ref, gl_ref, wg_ref, bg_ref, ones_ref, ll_ref, o_ref,
                       st_ref, st_scr, *, n_inner, scale):
    @pl.when(pl.program_id(2) == 0)
    def _():
        st_scr[...] = jnp.zeros_like(st_scr)

    c = GLA_CHUNK
    ll = ll_ref[...]
    g = _log_sigmoid(_dot(gl_ref[...].astype(BF16), wg_ref[...]) + bg_ref[...]) * (1.0 / GLA_GATE_NORM)
    chunks = [slice(ci * c, (ci + 1) * c) for ci in range(n_inner)]
    bs = [sum(_dot(ll, piece) for piece in _split_bf16(g[sl], 3)) for sl in chunks]
    qs = [q_ref[sl, :] * scale for sl in chunks]
    ks = [k_ref[sl, :] for sl in chunks]
    atts = _gla_intra(qs, ks, bs, ones_ref[...])
    st = st_scr[...]
    for ci, sl in enumerate(chunks):
        q, k, b = qs[ci], ks[ci], bs[ci]
        v16 = v_ref[sl, :].astype(BF16)
        o_ref[sl, :] = (_dot_nt((q * jnp.exp(b)).astype(BF16), st.astype(BF16))
                        + _dot(atts[ci].astype(BF16), v16))
        b_last = b[c - 1:c]
        kdec = (k * jnp.exp(b_last - b)).astype(BF16)
        st = st * jnp.exp(b_last) + _dot_tn(v16, kdec)
    st_scr[...] = st
    st_ref[...] = st


def _gla_prompt(proj, glow, wgu, b_gate, ones, batch, seq, heads, dk, dv, q_col, k_col, v_col,
                n_inner=4):
    m = proj.shape[0]
    cb = n_inner * GLA_CHUNK
    nc = seq // cb
    kernel = functools.partial(_gla_prompt_kernel, n_inner=n_inner, scale=dk ** -0.5)
    return pl.pallas_call(
        kernel,
        grid=(batch, heads, nc),
        in_specs=[pl.BlockSpec((cb, dk), lambda b, h, c: (b * nc + c, q_col // dk + h)),
                  pl.BlockSpec((cb, dk), lambda b, h, c: (b * nc + c, k_col // dk + h)),
                  pl.BlockSpec((cb, dv), lambda b, h, c: (b * nc + c, v_col // dv + h)),
                  pl.BlockSpec((cb, LANES), lambda b, h, c: (b * nc + c, 0)),
                  pl.BlockSpec((LANES, dk), lambda b, h, c: (0, h)),
                  pl.BlockSpec((1, dk), lambda b, h, c: (0, h)),
                  pl.BlockSpec((LANES, LANES), lambda b, h, c: (0, 0)),
                  pl.BlockSpec((GLA_CHUNK, GLA_CHUNK), lambda b, h, c: (0, 0))],
        out_specs=[pl.BlockSpec((cb, dv), lambda b, h, c: (b * nc + c, h)),
                   pl.BlockSpec((None, None, dv, dk), lambda b, h, c: (b, h, 0, 0))],
        out_shape=[jax.ShapeDtypeStruct((m, heads * dv), F32),
                   jax.ShapeDtypeStruct((batch, heads, dv, dk), F32)],
        scratch_shapes=[pltpu.VMEM((dv, dk), F32)],
        compiler_params=_params("arbitrary", "arbitrary", "arbitrary"),
        name="gla_prompt",
    )(proj, proj, proj, glow, wgu, b_gate, ones, _gla_prefix_matrix())


def _gla_step_kernel(q_ref, k_ref, v_ref, gl_ref, wg_ref, bg_ref, s_ref, o_ref, so_ref, *,
                     heads, dk, dv, scale):
    g = _log_sigmoid(_dot(gl_ref[...].astype(BF16), wg_ref[...]) + bg_ref[...]) * (1.0 / GLA_GATE_NORM)
    q = q_ref[...] * scale
    k = k_ref[...]
    v = v_ref[...]
    eye = (lax.broadcasted_iota(jnp.int32, (dk, dk), 0)
           == lax.broadcasted_iota(jnp.int32, (dk, dk), 1))

    def column(x_row):
        return jnp.sum(jnp.where(eye, x_row, 0.0), axis=1, keepdims=True)

    for h in range(heads):
        ksl = slice(h * dk, (h + 1) * dk)
        vsl = slice(h * dv, (h + 1) * dv)
        s_new = column(jnp.exp(g[:, ksl])) * s_ref[h] + column(k[:, ksl]) * v[:, vsl]
        so_ref[h] = s_new
        o_ref[:, vsl] = jnp.sum(column(q[:, ksl]) * s_new, axis=0, keepdims=True)


def _gla_step(proj, glow, wgu, b_gate, state, dk, dv, q_col, k_col, v_col):
    nb, heads = state.shape[0], state.shape[1]
    kw, vw = heads * dk, heads * dv
    kernel = functools.partial(_gla_step_kernel, heads=heads, dk=dk, dv=dv, scale=dk ** -0.5)
    proj3 = proj.reshape(nb, 1, proj.shape[1])
    glow3 = glow.reshape(nb, 1, LANES)
    o, s_new = pl.pallas_call(
        kernel,
        grid=(nb,),
        in_specs=[pl.BlockSpec((None, 1, kw), lambda b: (b, 0, q_col // kw)),
                  pl.BlockSpec((None, 1, kw), lambda b: (b, 0, k_col // kw)),
                  pl.BlockSpec((None, 1, vw), lambda b: (b, 0, v_col // vw)),
                  pl.BlockSpec((None, 1, LANES), lambda b: (b, 0, 0)),
                  pl.BlockSpec((LANES, kw), lambda b: (0, 0)),
                  pl.BlockSpec((1, kw), lambda b: (0, 0)),
                  pl.BlockSpec((None, heads, dk, dv), lambda b: (b, 0, 0, 0))],
        out_specs=[pl.BlockSpec((None, 1, vw), lambda b: (b, 0, 0)),
                   pl.BlockSpec((None, heads, dk, dv), lambda b: (b, 0, 0, 0))],
        out_shape=[jax.ShapeDtypeStruct((nb, 1, vw), F32),
                   jax.ShapeDtypeStruct(state.shape, F32)],
        compiler_params=_params("arbitrary"),
        name="gla_step",
    )(proj3, proj3, proj3, glow3, wgu, b_gate, state)
    return o.reshape(nb, vw), s_new


def _out_proj_kernel(osb_ref, og_ref, rg_ref, x_ref, gsb_ref, ggla_ref, w_ref, gpost_ref,
                     gt_ref, gpre_ref, sc_ref, sh_ref, x1_ref, h2_ref, mix_scr, *, dh, dv):
    sbw = osb_ref.shape[1]
    for h in range(sbw // dh):
        sl = slice(h * dh, (h + 1) * dh)
        mix_scr[:, sl] = _rms(osb_ref[:, sl], gsb_ref[...]).astype(BF16)
    for h in range(og_ref.shape[1] // dv):
        sl = slice(h * dv, (h + 1) * dv)
        y = _rms(og_ref[:, sl], ggla_ref[...]) * _silu(rg_ref[:, sl])
        mix_scr[:, sbw + h * dv:sbw + (h + 1) * dv] = y.astype(BF16)
    m = _dot(mix_scr[...], w_ref[...])
    x1 = x_ref[...] + gt_ref[...] * _rms(m, gpost_ref[...])
    x1_ref[...] = x1
    h2 = _rms(x1, gpre_ref[...]) * (1.0 + sc_ref[...]) + sh_ref[...]
    h2_ref[...] = h2.astype(BF16)


def _out_proj(osb, og, proj, rg_col, x, g_sb, g_gla, w, g_post, gt, g_pre, sc, sh, tm,
              rows_per_group):
    m, d = x.shape
    sbw, gw = osb.shape[1], og.shape[1]
    dh, dv = g_sb.shape[1], g_gla.shape[1]
    r = gt.shape[1]
    mod_spec = pl.BlockSpec((None, r, d), lambda i: ((i * tm) // rows_per_group, 0, 0))
    row = lambda width: pl.BlockSpec((1, width), lambda i: (0, 0))
    kernel = functools.partial(_out_proj_kernel, dh=dh, dv=dv)
    return pl.pallas_call(
        kernel,
        grid=(m // tm,),
        in_specs=[pl.BlockSpec((tm, sbw), lambda i: (i, 0)),
                  pl.BlockSpec((tm, gw), lambda i: (i, 0)),
                  pl.BlockSpec((tm, gw), lambda i: (i, rg_col // gw)),
                  pl.BlockSpec((tm, d), lambda i: (i, 0)),
                  row(dh), row(dv),
                  pl.BlockSpec((sbw + gw, d), lambda i: (0, 0)),
                  row(d), mod_spec, row(d), mod_spec, mod_spec],
        out_specs=[pl.BlockSpec((tm, d), lambda i: (i, 0)),
                   pl.BlockSpec((tm, d), lambda i: (i, 0))],
        out_shape=[jax.ShapeDtypeStruct((m, d), F32),
                   jax.ShapeDtypeStruct((m, d), BF16)],
        scratch_shapes=[pltpu.VMEM((tm, sbw + gw), BF16)],
        compiler_params=_params("arbitrary"),
        name="out_proj",
    )(osb, og, proj, x, g_sb, g_gla, w, g_post, gt, g_pre, sc, sh)


def _ffn_kernel(h_ref, wg_ref, wu_ref, wd_ref, x1_ref, gpost_ref, gt_ref, y_ref, acc_scr):
    j = pl.program_id(1)

    @pl.when(j == 0)
    def _():
        acc_scr[...] = jnp.zeros_like(acc_scr)

    h = h_ref[...]
    t = _silu(_dot(h, wg_ref[...])) * _dot(h, wu_ref[...])
    acc_scr[...] += _dot(t.astype(BF16), wd_ref[...])

    @pl.when(j == pl.num_programs(1) - 1)
    def _():
        y_ref[...] = x1_ref[...] + gt_ref[...] * _rms(acc_scr[...], gpost_ref[...])


def _ffn(h2, wg, wu, wd, x1, g_post, gt, tm, rows_per_group, tf=512):
    m, d = x1.shape
    f = wg.shape[1]
    r = gt.shape[1]
    return pl.pallas_call(
        _ffn_kernel,
        grid=(m // tm, f // tf),
        in_specs=[pl.BlockSpec((tm, d), lambda i, j: (i, 0)),
                  pl.BlockSpec((d, tf), lambda i, j: (0, j)),
                  pl.BlockSpec((d, tf), lambda i, j: (0, j)),
                  pl.BlockSpec((tf, d), lambda i, j: (j, 0)),
                  pl.BlockSpec((tm, d), lambda i, j: (i, 0)),
                  pl.BlockSpec((1, d), lambda i, j: (0, 0)),
                  pl.BlockSpec((None, r, d), lambda i, j: ((i * tm) // rows_per_group, 0, 0))],
        out_specs=pl.BlockSpec((tm, d), lambda i, j: (i, 0)),
        out_shape=jax.ShapeDtypeStruct((m, d), F32),
        scratch_shapes=[pltpu.VMEM((tm, d), F32)],
        compiler_params=_params("arbitrary", "arbitrary"),
        name="ffn",
    )(h2, wg, wu, wd, x1, g_post, gt)


def _suffix_sum_matrix():
    j = lax.broadcasted_iota(jnp.int32, (2 * KEY_BLOCK, 2 * KEY_BLOCK), 0) % KEY_BLOCK
    s = lax.broadcasted_iota(jnp.int32, (2 * KEY_BLOCK, 2 * KEY_BLOCK), 1)
    return jnp.where((j > s) | (s >= KEY_BLOCK), 1.0, 0.0).astype(BF16)


def kernel(x_prompt, x_sample, c_prompt, c_sample, cache_k, cache_v, page_table, state_gla, w_ada, b_ada, g_pre_mix, w_in, b_sb, w_gate_up, b_gate, g_sb_out, g_gla_out, w_out, g_post_mix, g_pre_ffn, w_ffn_gate, w_ffn_up, w_ffn_down, g_post_ffn):
    batch, seq, d = x_prompt.shape
    nb = x_sample.shape[0]
    depth, n_pool, page, sb_heads, dh = cache_k.shape
    gla_heads, dk, dv = state_gla.shape[2:]
    rank = w_gate_up.shape[1]
    assert depth == 1 and x_sample.shape[1] == 1
    sbw, kw, vw = sb_heads * dh, gla_heads * dk, gla_heads * dv
    assert 2 * kw == sbw and vw == sbw
    q_col, k_col, v_col, rg_col = 0, kw, sbw, 2 * sbw
    main = 3 * sbw + 2 * kw + 2 * vw

    row = lambda a: a.reshape(1, -1)
    uo = _suffix_sum_matrix()
    ones = jnp.ones((LANES, LANES), BF16)

    w_in0 = w_in[0]
    w_main = w_in0.astype(BF16)
    w_low = jnp.pad(w_in0[:, main:], ((0, 0), (0, LANES - rank))).astype(BF16)
    wgu = jnp.pad(w_gate_up[0], ((0, LANES - rank), (0, 0))).astype(BF16)
    w_o = w_out[0].astype(BF16)
    w_fg, w_fu, w_fd = (w[0].astype(BF16) for w in (w_ffn_gate, w_ffn_up, w_ffn_down))

    n_c = batch + nb
    c_all = jnp.pad(jnp.concatenate([c_prompt, c_sample], axis=0), ((0, -n_c % SUBLANES), (0, 0)))
    mod = _ada(c_all, w_ada[0], row(b_ada[0]))
    mod_p = [a.reshape(batch, 1, d) for a in jnp.split(mod[:batch], 6, axis=-1)]
    mod_s = [a.reshape(1, nb, d) for a in jnp.split(mod[batch:n_c], 6, axis=-1)]

    gains = dict(g_sb=row(g_sb_out[0]), g_gla=row(g_gla_out[0]), g_post=row(g_post_mix[0]),
                 g_pre=row(g_pre_ffn[0]))

    def rest_of_layer(x2, osb, og, proj, mods, tm, rows_per_group, ffn_tm):
        sh1, sc1, gt1, sh2, sc2, gt2 = mods
        x1, h2 = _out_proj(osb, og, proj, rg_col, x2, gains["g_sb"], gains["g_gla"], w_o,
                           gains["g_post"], gt1, gains["g_pre"], sc2, sh2, tm, rows_per_group)
        return _ffn(h2, w_fg, w_fu, w_fd, x1, row(g_post_ffn[0]), gt2, ffn_tm, rows_per_group)

    xp = x_prompt.reshape(batch * seq, d)
    xs = x_sample.reshape(nb, d)
    qkv_p, k_p, v_p, gla_p, glow_p = _in_proj(xp, row(g_pre_mix[0]), mod_p[1], mod_p[0], w_main, w_low,
                                              512, seq, sb_heads, dh)
    qkv_s, k_s, v_s, gla_s, glow_s = _in_proj(xs, row(g_pre_mix[0]), mod_s[1], mod_s[0], w_main, w_low,
                                              nb, nb, sb_heads, dh)

    bias_rows = jnp.broadcast_to(b_sb[0][:, None], (sb_heads, KEY_BLOCK))
    osb_p, osb_s = _sb_attention(qkv_p, b_sb[0], uo, qkv_s[0].reshape(nb, sb_heads, dh),
                                 cache_k[0].reshape(n_pool, page * sb_heads, dh),
                                 cache_v[0].reshape(n_pool, page * sb_heads, dh),
                                 page_table, bias_rows, batch, seq, sb_heads, dh, tq=512)

    og_p, st_p = _gla_prompt(gla_p, glow_p, wgu, row(b_gate[0]), ones, batch, seq, gla_heads,
                             dk, dv, q_col, k_col, v_col)
    y_p = rest_of_layer(xp, osb_p, og_p, gla_p, mod_p, 256, seq, 512)
    og_s, st_s = _gla_step(gla_s, glow_s, wgu, row(b_gate[0]), state_gla[0], dk, dv,
                           q_col, k_col, v_col)
    y_s = rest_of_layer(xs, osb_s.reshape(nb, sbw), og_s, gla_s, mod_s, nb, nb, nb)

    kv = lambda a, n: a.reshape(1, n, -1, sb_heads, dh)
    return (y_p.reshape(batch, seq, d), y_s.reshape(nb, 1, d),
            kv(k_p, batch), kv(v_p, batch),
            jnp.swapaxes(st_p, -1, -2)[None],
            kv(k_s, nb), kv(v_s, nb), st_s[None])
```

```python
import functools

import jax
import jax.numpy as jnp
from jax import lax
from jax.experimental import pallas as pl
from jax.experimental.pallas import tpu as pltpu

F32 = jnp.float32
BF16 = jnp.bfloat16

LANES = 128
SUBLANES = 8
VMEM_LIMIT_BYTES = 52 * 1024 * 1024

EPS = 1e-6
LOG2E = 1.4426950408889634
GLA_GATE_NORM = 16.0
GLA_CHUNK = 64
GLA_SUB = 8
KEY_BLOCK = 128


def _params(*sem):
    return pltpu.CompilerParams(dimension_semantics=sem, vmem_limit_bytes=VMEM_LIMIT_BYTES)


def _log_sigmoid(z):
    return jnp.minimum(z, 0.0) - jnp.log(1.0 + jnp.exp(-jnp.abs(z)))


def _log2_sigmoid(z2):
    return jnp.minimum(z2, 0.0) - jnp.log(1.0 + jnp.exp2(-jnp.abs(z2))) * LOG2E


def _silu(x):
    return x * (1.0 / (1.0 + jnp.exp(-x)))


def _rms(x, g):
    ms = jnp.mean(x * x, axis=-1, keepdims=True)
    return x * lax.rsqrt(ms + EPS) * g


def _split_bf16(x, pieces):
    out = []
    for _ in range(pieces - 1):
        p = x.astype(BF16)
        out.append(p)
        x = x - p.astype(F32)
    out.append(x.astype(BF16))
    return out


def _dot(a, b):
    return jnp.dot(a, b, preferred_element_type=F32)


def _dot_nt(a, b):
    return lax.dot_general(a, b, (((1,), (1,)), ((), ())), preferred_element_type=F32)


def _dot_tn(a, b):
    return lax.dot_general(a, b, (((0,), (0,)), ((), ())), preferred_element_type=F32)


def _ada_kernel(c_ref, w_ref, b_ref, o_ref):
    a = _silu(c_ref[...]).astype(BF16)
    o_ref[...] = _dot(a, w_ref[...].astype(BF16)) + b_ref[...]


def _ada(c, w, b, tn=1024):
    m, d = c.shape
    n = w.shape[1]
    return pl.pallas_call(
        _ada_kernel,
        grid=(n // tn,),
        in_specs=[pl.BlockSpec((m, d), lambda j: (0, 0)),
                  pl.BlockSpec((d, tn), lambda j: (0, j)),
                  pl.BlockSpec((1, tn), lambda j: (0, j))],
        out_specs=pl.BlockSpec((m, tn), lambda j: (0, j)),
        out_shape=jax.ShapeDtypeStruct((m, n), F32),
        compiler_params=_params("arbitrary"),
        name="ada_mod",
    )(c, w, b)


def _in_proj_kernel(x_ref, g_ref, sc_ref, sh_ref, w_ref, wl_ref, qkv_ref, krow_ref, vrow_ref,
                    gla_ref, ol_ref, h_scr, *, heads, dh, q_scale):
    j = pl.program_id(1)

    @pl.when(j == 0)
    def _():
        h = _rms(x_ref[...], g_ref[...]) * (1.0 + sc_ref[...]) + sh_ref[...]
        h_scr[...] = h.astype(BF16)
        ol_ref[...] = _dot(h_scr[...], wl_ref[...])

    def product():
        return _dot(h_scr[...], w_ref[...])

    @pl.when(j == 0)
    def _():
        qkv_ref[...] = (product() * q_scale).astype(BF16)

    for tile, rows_ref in ((1, krow_ref), (2, vrow_ref)):
        @pl.when(j == tile)
        def _(rows_ref=rows_ref):
            res = product()
            qkv_ref[...] = res.astype(BF16)
            for h in range(heads):
                rows_ref[pl.ds(h, res.shape[0], stride=heads), :] = res[:, h * dh:(h + 1) * dh]

    @pl.when(j >= 3)
    def _():
        gla_ref[...] = product()


def _in_proj(x, g, sc, sh, w, wl, tm, rows_per_group, heads, dh):
    m, d = x.shape
    tn = heads * dh
    r = sc.shape[1]
    mod_spec = pl.BlockSpec((None, r, d), lambda i, j: ((i * tm) // rows_per_group, 0, 0))
    kernel = functools.partial(_in_proj_kernel, heads=heads, dh=dh, q_scale=dh ** -0.5 * LOG2E)
    return pl.pallas_call(
        kernel,
        grid=(m // tm, 6),
        in_specs=[pl.BlockSpec((tm, d), lambda i, j: (i, 0)),
                  pl.BlockSpec((1, d), lambda i, j: (0, 0)),
                  mod_spec, mod_spec,
                  pl.BlockSpec((d, tn), lambda i, j: (0, j)),
                  pl.BlockSpec((d, LANES), lambda i, j: (0, 0))],
        out_specs=[pl.BlockSpec((None, tm, tn), lambda i, j: (jnp.minimum(j, 2), i, 0)),
                   pl.BlockSpec((tm * heads, dh), lambda i, j: (i, 0)),
                   pl.BlockSpec((tm * heads, dh), lambda i, j: (i, 0)),
                   pl.BlockSpec((tm, tn), lambda i, j: (i, jnp.clip(j - 3, 0, 2))),
                   pl.BlockSpec((tm, LANES), lambda i, j: (i, 0))],
        out_shape=[jax.ShapeDtypeStruct((3, m, tn), BF16),
                   jax.ShapeDtypeStruct((m * heads, dh), F32),
                   jax.ShapeDtypeStruct((m * heads, dh), F32),
                   jax.ShapeDtypeStruct((m, 3 * tn), F32),
                   jax.ShapeDtypeStruct((m, LANES), F32)],
        scratch_shapes=[pltpu.VMEM((tm, d), BF16)],
        compiler_params=_params("arbitrary", "arbitrary"),
        name="in_proj",
    )(x, g, sc, sh, w, wl)


def _sb_group(z, carry, uo, mask):
    n = z.shape[1] // KEY_BLOCK
    lb = _log2_sigmoid(z)
    lk = lb - z
    if mask is not None:
        lk = jnp.where(mask, lk, 0.0)
    hi, lo = _split_bf16(lk, 2)
    tails = [None] * n
    for s in reversed(range(n)):
        sl = slice(s * KEY_BLOCK, (s + 1) * KEY_BLOCK)
        ts = _dot(jnp.concatenate([hi[:, sl], lo[:, sl]], axis=1), uo)
        tails[s] = ts[:, :KEY_BLOCK] + carry
        carry = carry + ts[:, KEY_BLOCK:]
    a = jnp.exp2(lb + jnp.concatenate(tails, axis=1))
    if mask is not None:
        a = jnp.where(mask, a, 0.0)
    return a.astype(BF16), carry


def _sb_decode_step(q_ref, bias_ref, uo_ref, gather_ref, spread_ref, k_refs, v_refs, carry_scr,
                    acc_scr, heads):
    pages = len(k_refs)
    q = q_ref[...]
    rows = k_refs[0].shape[0]
    own = (lax.broadcasted_iota(jnp.int32, (pages * heads, rows), 1) % heads
           == lax.broadcasted_iota(jnp.int32, (pages * heads, rows), 0) % heads)
    z_all = jnp.concatenate([_dot_nt(q, r[...].astype(BF16)) for r in k_refs], axis=0)
    pieces = _split_bf16(jnp.where(own, z_all, 0.0), 3)
    zr = _dot(jnp.concatenate(pieces, axis=1), gather_ref[...])
    zr = zr + jnp.concatenate([bias_ref[...] * LOG2E] * pages, axis=0)
    lb = _log2_sigmoid(zr)
    hi, lo = _split_bf16(lb - zr, 2)
    ts = _dot(jnp.concatenate([hi, lo], axis=1), uo_ref[...])
    carry = carry_scr[...]
    tails = []
    for p in range(pages):
        tsp = ts[p * heads:(p + 1) * heads]
        tails.append(tsp[:, :KEY_BLOCK] + carry)
        carry = carry + tsp[:, KEY_BLOCK:]
    carry_scr[...] = carry
    a = jnp.exp2(lb + jnp.concatenate(tails, axis=0)).astype(BF16)
    a_rows = jnp.where(own, _dot(a, spread_ref[...]), 0.0)
    acc = acc_scr[...]
    for p in range(pages):
        acc = acc + _dot(a_rows[p * heads:(p + 1) * heads].astype(BF16), v_refs[p][...].astype(BF16))
    acc_scr[...] = acc


def _sb_attn_kernel(pt_ref, q_ref, k_ref, v_ref, bias_ref, uo_ref, qs_ref, bias_rows_ref,
                    gather_ref, spread_ref, *refs, tq, group, pages, heads):
    k_refs, v_refs = refs[:pages], refs[pages:2 * pages]
    o_ref, os_ref, carry_scr, acc_scr, dcarry_scr, dacc_scr = refs[2 * pages:]
    qi = pl.program_id(2)
    h = pl.program_id(1)
    q = q_ref[...]
    bias = bias_ref[h] * LOG2E
    uo = uo_ref[...]
    n_diag = tq // KEY_BLOCK

    @pl.when(qi == 0)
    def _():
        dcarry_scr[...] = jnp.zeros_like(dcarry_scr)
        dacc_scr[...] = jnp.zeros_like(dacc_scr)

    def visit(first_block, n, newest):
        rows = n * KEY_BLOCK
        start = pl.multiple_of(first_block * KEY_BLOCK, KEY_BLOCK)
        kb = k_ref[pl.ds(start, rows), :]
        vb = v_ref[pl.ds(start, rows), :]
        mask = None
        if newest:
            mask = (lax.broadcasted_iota(jnp.int32, (tq, rows), 1) + (start - qi * tq)
                    < lax.broadcasted_iota(jnp.int32, (tq, rows), 0))
            _sb_decode_step(qs_ref, bias_rows_ref, uo_ref, gather_ref, spread_ref, k_refs, v_refs,
                            dcarry_scr, dacc_scr, heads)
        a, carry = _sb_group(_dot_nt(q, kb) + bias, carry_scr[...], uo, mask)
        carry_scr[...] = carry
        acc_scr[...] += _dot(a, vb)

    carry_scr[...] = jnp.zeros_like(carry_scr)
    acc_scr[...] = jnp.zeros_like(acc_scr)
    n_tot = (qi + 1) * n_diag
    n_groups = n_tot // group
    leftover = lax.rem(n_tot, group)

    @pl.when(n_groups > 0)
    def _():
        visit(n_tot - group, group, True)

    def body(g, _):
        visit(n_tot - (g + 1) * group, group, False)
        return 0

    lax.fori_loop(1, n_groups, body, 0)
    size = group // 2
    while size >= n_diag:
        take = lax.rem(lax.div(leftover, size), 2) == 1
        is_first = jnp.logical_and(n_groups == 0, lax.div(leftover, 2 * size) == 0)
        for newest in (True, False):
            pl.when(jnp.logical_and(take, is_first == newest))(
                functools.partial(visit, lax.rem(leftover, size), size, newest))
        size //= 2
    o_ref[...] = acc_scr[...]

    @pl.when(qi == pl.num_programs(2) - 1)
    def _():
        os_ref[...] = dacc_scr[...]


def _sb_attention(qkv, b_sb, uo, q_s, cache_k, cache_v, page_table, bias_rows, batch, seq, heads,
                  dh, tq, group=8):
    m = qkv.shape[1]
    nq = seq // tq
    nb, n_pages = page_table.shape
    rows = cache_k.shape[1]
    assert nb == batch * heads and n_pages % nq == 0 and rows == KEY_BLOCK * heads
    pages = n_pages // nq
    kernel = functools.partial(_sb_attn_kernel, tq=tq, group=group, pages=pages, heads=heads)
    key_of_row = lax.broadcasted_iota(jnp.int32, (rows, KEY_BLOCK), 0) // heads
    gather1 = jnp.where(key_of_row == lax.broadcasted_iota(jnp.int32, (rows, KEY_BLOCK), 1), 1.0, 0.0)
    gather = jnp.concatenate([gather1] * 3, axis=0).astype(BF16)
    spread = gather1.T.astype(BF16)

    def page_spec(j):
        return pl.BlockSpec(
            (None, rows, dh),
            lambda b, h, i, pt, j=j: (pt[b * heads + h, n_pages - 1 - (i * pages + j)], 0, 0))

    const = lambda a: pl.BlockSpec(a.shape, lambda b, h, i, pt: (0,) * a.ndim)
    grid_spec = pltpu.PrefetchScalarGridSpec(
        num_scalar_prefetch=1,
        grid=(batch, heads, nq),
        in_specs=[pl.BlockSpec((None, tq, dh), lambda b, h, i, pt: (0, b * nq + i, h)),
                  pl.BlockSpec((None, seq, dh), lambda b, h, i, pt: (1, b, h)),
                  pl.BlockSpec((None, seq, dh), lambda b, h, i, pt: (2, b, h)),
                  pl.BlockSpec(memory_space=pltpu.SMEM),
                  const(uo),
                  pl.BlockSpec((None, heads, dh), lambda b, h, i, pt: (b * heads + h, 0, 0)),
                  const(bias_rows), const(gather), const(spread)]
                 + [page_spec(j) for j in range(pages)] * 2,
        out_specs=[pl.BlockSpec((tq, dh), lambda b, h, i, pt: (b * nq + i, h)),
                   pl.BlockSpec((None, heads, dh), lambda b, h, i, pt: (b * heads + h, 0, 0))],
        scratch_shapes=[pltpu.VMEM((tq, KEY_BLOCK), F32), pltpu.VMEM((tq, dh), F32),
                        pltpu.VMEM((heads, KEY_BLOCK), F32), pltpu.VMEM((heads, dh), F32)],
    )
    return pl.pallas_call(
        kernel,
        grid_spec=grid_spec,
        out_shape=[jax.ShapeDtypeStruct((m, heads * dh), F32),
                   jax.ShapeDtypeStruct((nb, heads, dh), F32)],
        compiler_params=_params("arbitrary", "arbitrary", "arbitrary"),
        name="sb_attention",
    )(page_table, qkv, qkv, qkv, b_sb, uo, q_s, bias_rows, gather, spread,
      *([cache_k] * pages), *([cache_v] * pages))


def _gla_prefix_matrix():
    t = lax.broadcasted_iota(jnp.int32, (GLA_CHUNK, GLA_CHUNK), 0)
    s = lax.broadcasted_iota(jnp.int32, (GLA_CHUNK, GLA_CHUNK), 1)
    return jnp.where(s <= t, 1.0, 0.0).astype(BF16)


def _gla_intra(qs, ks, bs, ones_bf16):
    n = len(qs)
    c = GLA_CHUNK
    t_i = lax.broadcasted_iota(jnp.int32, (c, c), 0)
    s_i = lax.broadcasted_iota(jnp.int32, (c, c), 1)
    atts = [jnp.zeros((c, c), F32)] * n
    h = c // 2
    while h >= GLA_SUB:
        th, sh = t_i // h, s_i // h
        pair = ((th - sh - 1) | ((th & 1) ^ 1)) == 0
        for ci in range(n):
            q, k, b = qs[ci], ks[ci], bs[ci]
            ref = jnp.concatenate(
                [jnp.broadcast_to(b[j + h - 1:j + h], (2 * h, b.shape[1])) for j in range(0, c, 2 * h)],
                axis=0)
            ql = (q * jnp.exp(jnp.minimum(b - ref, 0.0))).astype(BF16)
            kl = (k * jnp.exp(jnp.minimum(ref - b, 0.0))).astype(BF16)
            atts[ci] = atts[ci] + jnp.where(pair, _dot_nt(ql, kl), 0.0)
        h //= 2
    prods = []
    for ci in range(n):
        q, k, b = qs[ci], ks[ci], bs[ci]
        for lo_ in range(0, c, GLA_SUB):
            qi, ki, bi = (a[lo_:lo_ + GLA_SUB] for a in (q, k, b))
            prods += [qi * ki[s:s + 1] * jnp.exp(jnp.minimum(bi - bi[s:s + 1], 0.0))
                      for s in range(GLA_SUB)]
    rsum = _dot(jnp.concatenate(prods, axis=0).astype(BF16), ones_bf16)
    lane = lax.broadcasted_iota(jnp.int32, (GLA_SUB, LANES), 1)
    trow = lax.broadcasted_iota(jnp.int32, (GLA_SUB, LANES), 0)
    for ci in range(n):
        strips = []
        for i, lo_ in enumerate(range(0, c, GLA_SUB)):
            base = (ci * (c // GLA_SUB) + i) * GLA_SUB * GLA_SUB
            strip = jnp.zeros((GLA_SUB, LANES), F32)
            for s in range(GLA_SUB):
                strip = jnp.where(lane == lo_ + s,
                                  rsum[base + s * GLA_SUB:base + (s + 1) * GLA_SUB], strip)
            strips.append(jnp.where(lane - lo_ <= trow, strip, 0.0)[:, :c])
        atts[ci] = atts[ci] + jnp.concatenate(strips, axis=0)
    return atts


def _gla_prompt_kernel(q_ref, k_ref, v_ref, gl_ref, wg_ref, bg_ref, ones_ref, ll_ref, o_ref,
                       st_ref, st_scr, *, n_inner, scale):
    @pl.when(pl.program_id(2) == 0)
    def _():
        st_scr[...] = jnp.zeros_like(st_scr)

    c = GLA_CHUNK
    ll = ll_ref[...]
    g = _log_sigmoid(_dot(gl_ref[...].astype(BF16), wg_ref[...]) + bg_ref[...]) * (1.0 / GLA_GATE_NORM)
    chunks = [slice(ci * c, (ci + 1) * c) for ci in range(n_inner)]
    bs = [sum(_dot(ll, piece) for piece in _split_bf16(g[sl], 3)) for sl in chunks]
    qs = [q_ref[sl, :] * scale for sl in chunks]
    ks = [k_ref[sl, :] for sl in chunks]
    atts = _gla_intra(qs, ks, bs, ones_ref[...])
    st = st_scr[...]
    for ci, sl in enumerate(chunks):
        q, k, b = qs[ci], ks[ci], bs[ci]
        v16 = v_ref[sl, :].astype(BF16)
        o_ref[sl, :] = (_dot_nt((q * jnp.exp(b)).astype(BF16), st.astype(BF16))
                        + _dot(atts[ci].astype(BF16), v16))
        b_last = b[c - 1:c]
        kdec = (k * jnp.exp(b_last - b)).astype(BF16)
        st = st * jnp.exp(b_last) + _dot_tn(v16, kdec)
    st_scr[...] = st
    st_ref[...] = st


def _gla_prompt(proj, glow, wgu, b_gate, ones, batch, seq, heads, dk, dv, q_col, k_col, v_col,
                n_inner=16):
    m = proj.shape[0]
    cb = n_inner * GLA_CHUNK
    nc = seq // cb
    kernel = functools.partial(_gla_prompt_kernel, n_inner=n_inner, scale=dk ** -0.5)
    return pl.pallas_call(
        kernel,
        grid=(batch, heads, nc),
        in_specs=[pl.BlockSpec((cb, dk), lambda b, h, c: (b * nc + c, q_col // dk + h)),
                  pl.BlockSpec((cb, dk), lambda b, h, c: (b * nc + c, k_col // dk + h)),
                  pl.BlockSpec((cb, dv), lambda b, h, c: (b * nc + c, v_col // dv + h)),
                  pl.BlockSpec((cb, LANES), lambda b, h, c: (b * nc + c, 0)),
                  pl.BlockSpec((LANES, dk), lambda b, h, c: (0, h)),
                  pl.BlockSpec((1, dk), lambda b, h, c: (0, h)),
                  pl.BlockSpec((LANES, LANES), lambda b, h, c: (0, 0)),
                  pl.BlockSpec((GLA_CHUNK, GLA_CHUNK), lambda b, h, c: (0, 0))],
        out_specs=[pl.BlockSpec((cb, dv), lambda b, h, c: (b * nc + c, h)),
                   pl.BlockSpec((None, None, dv, dk), lambda b, h, c: (b, h, 0, 0))],
        out_shape=[jax.ShapeDtypeStruct((m, heads * dv), F32),
                   jax.ShapeDtypeStruct((batch, heads, dv, dk), F32)],
        scratch_shapes=[pltpu.VMEM((dv, dk), F32)],
        compiler_params=_params("arbitrary", "arbitrary", "arbitrary"),
        name="gla_prompt",
    )(proj, proj, proj, glow, wgu, b_gate, ones, _gla_prefix_matrix())


def _gla_step_kernel(q_ref, k_ref, v_ref, gl_ref, wg_ref, bg_ref, s_ref, o_ref, so_ref, *,
                     heads, dk, dv, scale):
    eye = (lax.broadcasted_iota(jnp.int32, (dk, dk), 0)
           == lax.broadcasted_iota(jnp.int32, (dk, dk), 1))

    def column(x_row):
        return jnp.sum(jnp.where(eye, x_row, 0.0), axis=1, keepdims=True)

    for i in range(q_ref.shape[0]):
        g = _log_sigmoid(_dot(gl_ref[i].astype(BF16), wg_ref[...]) + bg_ref[...]) * (1.0 / GLA_GATE_NORM)
        q = q_ref[i] * scale
        k = k_ref[i]
        v = v_ref[i]
        for h in range(heads):
            ksl = slice(h * dk, (h + 1) * dk)
            vsl = slice(h * dv, (h + 1) * dv)
            s_new = column(jnp.exp(g[:, ksl])) * s_ref[i, h] + column(k[:, ksl]) * v[:, vsl]
            so_ref[i, h] = s_new
            o_ref[i, :, vsl] = jnp.sum(column(q[:, ksl]) * s_new, axis=0, keepdims=True)


def _gla_step(proj, glow, wgu, b_gate, state, dk, dv, q_col, k_col, v_col, seqs=4):
    nb, heads = state.shape[0], state.shape[1]
    kw, vw = heads * dk, heads * dv
    kernel = functools.partial(_gla_step_kernel, heads=heads, dk=dk, dv=dv, scale=dk ** -0.5)
    proj3 = proj.reshape(nb, 1, proj.shape[1])
    glow3 = glow.reshape(nb, 1, LANES)
    o, s_new = pl.pallas_call(
        kernel,
        grid=(nb // seqs,),
        in_specs=[pl.BlockSpec((seqs, 1, kw), lambda b: (b, 0, q_col // kw)),
                  pl.BlockSpec((seqs, 1, kw), lambda b: (b, 0, k_col // kw)),
                  pl.BlockSpec((seqs, 1, vw), lambda b: (b, 0, v_col // vw)),
                  pl.BlockSpec((seqs, 1, LANES), lambda b: (b, 0, 0)),
                  pl.BlockSpec((LANES, kw), lambda b: (0, 0)),
                  pl.BlockSpec((1, kw), lambda b: (0, 0)),
                  pl.BlockSpec((seqs, heads, dk, dv), lambda b: (b, 0, 0, 0))],
        out_specs=[pl.BlockSpec((seqs, 1, vw), lambda b: (b, 0, 0)),
                   pl.BlockSpec((seqs, heads, dk, dv), lambda b: (b, 0, 0, 0))],
        out_shape=[jax.ShapeDtypeStruct((nb, 1, vw), F32),
                   jax.ShapeDtypeStruct(state.shape, F32)],
        compiler_params=_params("arbitrary"),
        name="gla_step",
    )(proj3, proj3, proj3, glow3, wgu, b_gate, state)
    return o.reshape(nb, vw), s_new


def _out_proj_kernel(osb_ref, og_ref, rg_ref, x_ref, gsb_ref, ggla_ref, w_ref, gpost_ref,
                     gt_ref, gpre_ref, sc_ref, sh_ref, x1_ref, h2_ref, mix_scr, *, dh, dv):
    sbw = osb_ref.shape[1]
    for h in range(sbw // dh):
        sl = slice(h * dh, (h + 1) * dh)
        mix_scr[:, sl] = _rms(osb_ref[:, sl], gsb_ref[...]).astype(BF16)
    for h in range(og_ref.shape[1] // dv):
        sl = slice(h * dv, (h + 1) * dv)
        y = _rms(og_ref[:, sl], ggla_ref[...]) * _silu(rg_ref[:, sl])
        mix_scr[:, sbw + h * dv:sbw + (h + 1) * dv] = y.astype(BF16)
    m = _dot(mix_scr[...], w_ref[...])
    x1 = x_ref[...] + gt_ref[...] * _rms(m, gpost_ref[...])
    x1_ref[...] = x1
    h2 = _rms(x1, gpre_ref[...]) * (1.0 + sc_ref[...]) + sh_ref[...]
    h2_ref[...] = h2.astype(BF16)


def _out_proj(osb, og, proj, rg_col, x, g_sb, g_gla, w, g_post, gt, g_pre, sc, sh, tm,
              rows_per_group):
    m, d = x.shape
    sbw, gw = osb.shape[1], og.shape[1]
    dh, dv = g_sb.shape[1], g_gla.shape[1]
    r = gt.shape[1]
    mod_spec = pl.BlockSpec((None, r, d), lambda i: ((i * tm) // rows_per_group, 0, 0))
    row = lambda width: pl.BlockSpec((1, width), lambda i: (0, 0))
    kernel = functools.partial(_out_proj_kernel, dh=dh, dv=dv)
    return pl.pallas_call(
        kernel,
        grid=(m // tm,),
        in_specs=[pl.BlockSpec((tm, sbw), lambda i: (i, 0)),
                  pl.BlockSpec((tm, gw), lambda i: (i, 0)),
                  pl.BlockSpec((tm, gw), lambda i: (i, rg_col // gw)),
                  pl.BlockSpec((tm, d), lambda i: (i, 0)),
                  row(dh), row(dv),
                  pl.BlockSpec((sbw + gw, d), lambda i: (0, 0)),
                  row(d), mod_spec, row(d), mod_spec, mod_spec],
        out_specs=[pl.BlockSpec((tm, d), lambda i: (i, 0)),
                   pl.BlockSpec((tm, d), lambda i: (i, 0))],
        out_shape=[jax.ShapeDtypeStruct((m, d), F32),
                   jax.ShapeDtypeStruct((m, d), BF16)],
        scratch_shapes=[pltpu.VMEM((tm, sbw + gw), BF16)],
        compiler_params=_params("arbitrary"),
        name="out_proj",
    )(osb, og, proj, x, g_sb, g_gla, w, g_post, gt, g_pre, sc, sh)


def _ffn_kernel(h_ref, wg_ref, wu_ref, wd_ref, x1_ref, gpost_ref, gt_ref, y_ref, acc_scr):
    j = pl.program_id(1)
    last = pl.num_programs(1) - 1

    def part():
        h = h_ref[...]
        t = _silu(_dot(h, wg_ref[...])) * _dot(h, wu_ref[...])
        return _dot(t.astype(BF16), wd_ref[...])

    @pl.when(j == 0)
    def _():
        acc_scr[...] = part()

    @pl.when(jnp.logical_and(j > 0, j < last))
    def _():
        acc_scr[...] += part()

    @pl.when(j == last)
    def _():
        y_ref[...] = x1_ref[...] + gt_ref[...] * _rms(acc_scr[...] + part(), gpost_ref[...])


def _ffn(h2, wg, wu, wd, x1, g_post, gt, tm, rows_per_group, tf=512):
    m, d = x1.shape
    f = wg.shape[1]
    r = gt.shape[1]
    assert f // tf >= 2
    return pl.pallas_call(
        _ffn_kernel,
        grid=(m // tm, f // tf),
        in_specs=[pl.BlockSpec((tm, d), lambda i, j: (i, 0)),
                  pl.BlockSpec((d, tf), lambda i, j: (0, j)),
                  pl.BlockSpec((d, tf), lambda i, j: (0, j)),
                  pl.BlockSpec((tf, d), lambda i, j: (j, 0)),
                  pl.BlockSpec((tm, d), lambda i, j: (i, 0)),
                  pl.BlockSpec((1, d), lambda i, j: (0, 0)),
                  pl.BlockSpec((None, r, d), lambda i, j: ((i * tm) // rows_per_group, 0, 0))],
        out_specs=pl.BlockSpec((tm, d), lambda i, j: (i, 0)),
        out_shape=jax.ShapeDtypeStruct((m, d), F32),
        scratch_shapes=[pltpu.VMEM((tm, d), F32)],
        compiler_params=_params("arbitrary", "arbitrary"),
        name="ffn",
    )(h2, wg, wu, wd, x1, g_post, gt)


def _suffix_sum_matrix():
    j = lax.broadcasted_iota(jnp.int32, (2 * KEY_BLOCK, 2 * KEY_BLOCK), 0) % KEY_BLOCK
    s = lax.broadcasted_iota(jnp.int32, (2 * KEY_BLOCK, 2 * KEY_BLOCK), 1)
    return jnp.where((j > s) | (s >= KEY_BLOCK), 1.0, 0.0).astype(BF16)


def kernel(x_prompt, x_sample, c_prompt, c_sample, cache_k, cache_v, page_table, state_gla, w_ada, b_ada, g_pre_mix, w_in, b_sb, w_gate_up, b_gate, g_sb_out, g_gla_out, w_out, g_post_mix, g_pre_ffn, w_ffn_gate, w_ffn_up, w_ffn_down, g_post_ffn):
    batch, seq, d = x_prompt.shape
    nb = x_sample.shape[0]
    depth, n_pool, page, sb_heads, dh = cache_k.shape
    gla_heads, dk, dv = state_gla.shape[2:]
    rank = w_gate_up.shape[1]
    assert depth == 1 and x_sample.shape[1] == 1
    sbw, kw, vw = sb_heads * dh, gla_heads * dk, gla_heads * dv
    assert 2 * kw == sbw and vw == sbw
    q_col, k_col, v_col, rg_col = 0, kw, sbw, 2 * sbw
    main = 3 * sbw + 2 * kw + 2 * vw

    row = lambda a: a.reshape(1, -1)
    uo = _suffix_sum_matrix()
    ones = jnp.ones((LANES, LANES), BF16)

    w_in0 = w_in[0]
    w_main = w_in0.astype(BF16)
    w_low = jnp.pad(w_in0[:, main:], ((0, 0), (0, LANES - rank))).astype(BF16)
    wgu = jnp.pad(w_gate_up[0], ((0, LANES - rank), (0, 0))).astype(BF16)
    w_o = w_out[0].astype(BF16)
    w_fg, w_fu, w_fd = (w[0].astype(BF16) for w in (w_ffn_gate, w_ffn_up, w_ffn_down))

    n_c = batch + nb
    c_all = jnp.pad(jnp.concatenate([c_prompt, c_sample], axis=0), ((0, -n_c % SUBLANES), (0, 0)))
    mod = _ada(c_all, w_ada[0], row(b_ada[0]))
    mod_p = [a.reshape(batch, 1, d) for a in jnp.split(mod[:batch], 6, axis=-1)]
    mod_s = [a.reshape(1, nb, d) for a in jnp.split(mod[batch:n_c], 6, axis=-1)]

    gains = dict(g_sb=row(g_sb_out[0]), g_gla=row(g_gla_out[0]), g_post=row(g_post_mix[0]),
                 g_pre=row(g_pre_ffn[0]))

    def rest_of_layer(x2, osb, og, proj, mods, tm, rows_per_group, ffn_tm):
        sh1, sc1, gt1, sh2, sc2, gt2 = mods
        x1, h2 = _out_proj(osb, og, proj, rg_col, x2, gains["g_sb"], gains["g_gla"], w_o,
                           gains["g_post"], gt1, gains["g_pre"], sc2, sh2, tm, rows_per_group)
        return _ffn(h2, w_fg, w_fu, w_fd, x1, row(g_post_ffn[0]), gt2, ffn_tm, rows_per_group)

    xp = x_prompt.reshape(batch * seq, d)
    xs = x_sample.reshape(nb, d)
    qkv_p, k_p, v_p, gla_p, glow_p = _in_proj(xp, row(g_pre_mix[0]), mod_p[1], mod_p[0], w_main, w_low,
                                              512, seq, sb_heads, dh)
    qkv_s, k_s, v_s, gla_s, glow_s = _in_proj(xs, row(g_pre_mix[0]), mod_s[1], mod_s[0], w_main, w_low,
                                              nb, nb, sb_heads, dh)

    bias_rows = jnp.broadcast_to(b_sb[0][:, None], (sb_heads, KEY_BLOCK))
    osb_p, osb_s = _sb_attention(qkv_p, b_sb[0], uo, qkv_s[0].reshape(nb, sb_heads, dh),
                                 cache_k[0].reshape(n_pool, page * sb_heads, dh),
                                 cache_v[0].reshape(n_pool, page * sb_heads, dh),
                                 page_table, bias_rows, batch, seq, sb_heads, dh, tq=512)

    og_p, st_p = _gla_prompt(gla_p, glow_p, wgu, row(b_gate[0]), ones, batch, seq, gla_heads,
                             dk, dv, q_col, k_col, v_col)
    y_p = rest_of_layer(xp, osb_p, og_p, gla_p, mod_p, 256, seq, 512)
    og_s, st_s = _gla_step(gla_s, glow_s, wgu, row(b_gate[0]), state_gla[0], dk, dv,
                           q_col, k_col, v_col)
    y_s = rest_of_layer(xs, osb_s.reshape(nb, sbw), og_s, gla_s, mod_s, nb, nb, nb)

    kv = lambda a, n: a.reshape(1, n, -1, sb_heads, dh)
    return (y_p.reshape(batch, seq, d), y_s.reshape(nb, 1, d),
            kv(k_p, batch), kv(v_p, batch),
            jnp.swapaxes(st_p, -1, -2)[None],
            kv(k_s, nb), kv(v_s, nb), st_s[None])
```

```python
import functools

import jax
import jax.numpy as jnp
from jax import lax
from jax.experimental import pallas as pl
from jax.experimental.pallas import tpu as pltpu

F32 = jnp.float32
BF16 = jnp.bfloat16

LANES = 128
SUBLANES = 8
VMEM_LIMIT_BYTES = 52 * 1024 * 1024

EPS = 1e-6
LOG2E = 1.4426950408889634
GLA_GATE_NORM = 16.0
GLA_CHUNK = 64
GLA_SUB = 8
KEY_BLOCK = 128


def _params(*sem):
    return pltpu.CompilerParams(dimension_semantics=sem, vmem_limit_bytes=VMEM_LIMIT_BYTES)


def _log_sigmoid(z):
    return jnp.minimum(z, 0.0) - jnp.log(1.0 + jnp.exp(-jnp.abs(z)))


def _log2_sigmoid(z2):
    return jnp.minimum(z2, 0.0) - jnp.log(1.0 + jnp.exp2(-jnp.abs(z2))) * LOG2E


def _silu(x):
    return x * (1.0 / (1.0 + jnp.exp(-x)))


def _rms(x, g):
    ms = jnp.mean(x * x, axis=-1, keepdims=True)
    return x * lax.rsqrt(ms + EPS) * g


def _split_bf16(x, pieces):
    out = []
    for _ in range(pieces - 1):
        p = x.astype(BF16)
        out.append(p)
        x = x - p.astype(F32)
    out.append(x.astype(BF16))
    return out


def _dot(a, b):
    return jnp.dot(a, b, preferred_element_type=F32)


def _dot_nt(a, b):
    return lax.dot_general(a, b, (((1,), (1,)), ((), ())), preferred_element_type=F32)


def _dot_tn(a, b):
    return lax.dot_general(a, b, (((0,), (0,)), ((), ())), preferred_element_type=F32)


def _ada_kernel(c_ref, w_ref, b_ref, o_ref):
    a = _silu(c_ref[...]).astype(BF16)
    o_ref[...] = _dot(a, w_ref[...].astype(BF16)) + b_ref[...]


def _ada(c, w, b, tn=1024):
    m, d = c.shape
    n = w.shape[1]
    return pl.pallas_call(
        _ada_kernel,
        grid=(n // tn,),
        in_specs=[pl.BlockSpec((m, d), lambda j: (0, 0)),
                  pl.BlockSpec((d, tn), lambda j: (0, j)),
                  pl.BlockSpec((1, tn), lambda j: (0, j))],
        out_specs=pl.BlockSpec((m, tn), lambda j: (0, j)),
        out_shape=jax.ShapeDtypeStruct((m, n), F32),
        compiler_params=_params("arbitrary"),
        name="ada_mod",
    )(c, w, b)


def _prenorm(x_ref, g_ref, sc_ref, sh_ref, h_scr):
    h = _rms(x_ref[...], g_ref[...]) * (1.0 + sc_ref[...]) + sh_ref[...]
    h_scr[...] = h.astype(BF16)


def _in_proj_attn_kernel(x_ref, g_ref, sc_ref, sh_ref, w_ref, qkv_ref, krow_ref, vrow_ref, h_scr,
                         *, heads, dh, q_scale):
    j = pl.program_id(1)

    def product():
        return _dot(h_scr[...], w_ref[...])

    @pl.when(j == 0)
    def _():
        _prenorm(x_ref, g_ref, sc_ref, sh_ref, h_scr)
        qkv_ref[...] = (product() * q_scale).astype(BF16)

    for tile, rows_ref in ((1, krow_ref), (2, vrow_ref)):
        @pl.when(j == tile)
        def _(rows_ref=rows_ref):
            res = product()
            qkv_ref[...] = res.astype(BF16)
            for h in range(heads):
                rows_ref[pl.ds(h, res.shape[0], stride=heads), :] = res[:, h * dh:(h + 1) * dh]


def _in_proj_gla_kernel(x_ref, g_ref, sc_ref, sh_ref, w_ref, wl_ref, gla_ref, ol_ref, h_scr):
    @pl.when(pl.program_id(1) == 0)
    def _():
        _prenorm(x_ref, g_ref, sc_ref, sh_ref, h_scr)
        ol_ref[...] = _dot(h_scr[...], wl_ref[...])

    gla_ref[...] = _dot(h_scr[...], w_ref[...])


def _in_proj(x, g, sc, sh, w, wl, tm, rows_per_group, heads, dh):
    m, d = x.shape
    tn = heads * dh
    r = sc.shape[1]
    mod_spec = pl.BlockSpec((None, r, d), lambda i, j: ((i * tm) // rows_per_group, 0, 0))
    common = [pl.BlockSpec((tm, d), lambda i, j: (i, 0)),
              pl.BlockSpec((1, d), lambda i, j: (0, 0)),
              mod_spec, mod_spec]
    attn_kernel = functools.partial(_in_proj_attn_kernel, heads=heads, dh=dh,
                                    q_scale=dh ** -0.5 * LOG2E)
    qkv, krow, vrow = pl.pallas_call(
        attn_kernel,
        grid=(m // tm, 3),
        in_specs=common + [pl.BlockSpec((d, tn), lambda i, j: (0, j))],
        out_specs=[pl.BlockSpec((None, tm, tn), lambda i, j: (j, i, 0)),
                   pl.BlockSpec((tm * heads, dh), lambda i, j: (i, 0)),
                   pl.BlockSpec((tm * heads, dh), lambda i, j: (i, 0))],
        out_shape=[jax.ShapeDtypeStruct((3, m, tn), BF16),
                   jax.ShapeDtypeStruct((m * heads, dh), F32),
                   jax.ShapeDtypeStruct((m * heads, dh), F32)],
        scratch_shapes=[pltpu.VMEM((tm, d), BF16)],
        compiler_params=_params("arbitrary", "arbitrary"),
        name="in_proj_attn",
    )(x, g, sc, sh, w)
    gla, glow = pl.pallas_call(
        _in_proj_gla_kernel,
        grid=(m // tm, 3),
        in_specs=common + [pl.BlockSpec((d, tn), lambda i, j: (0, 3 + j)),
                           pl.BlockSpec((d, LANES), lambda i, j: (0, 0))],
        out_specs=[pl.BlockSpec((tm, tn), lambda i, j: (i, j)),
                   pl.BlockSpec((tm, LANES), lambda i, j: (i, 0))],
        out_shape=[jax.ShapeDtypeStruct((m, 3 * tn), F32),
                   jax.ShapeDtypeStruct((m, LANES), F32)],
        scratch_shapes=[pltpu.VMEM((tm, d), BF16)],
        compiler_params=_params("arbitrary", "arbitrary"),
        name="in_proj_gla",
    )(x, g, sc, sh, w, wl)
    return qkv, krow, vrow, gla, glow


def _sb_group(z, carry, uo, mask):
    n = z.shape[1] // KEY_BLOCK
    lb = _log2_sigmoid(z)
    lk = lb - z
    if mask is not None:
        lk = jnp.where(mask, lk, 0.0)
    hi, lo = _split_bf16(lk, 2)
    tails = [None] * n
    for s in reversed(range(n)):
        sl = slice(s * KEY_BLOCK, (s + 1) * KEY_BLOCK)
        ts = _dot(jnp.concatenate([hi[:, sl], lo[:, sl]], axis=1), uo)
        tails[s] = ts[:, :KEY_BLOCK] + carry
        carry = carry + ts[:, KEY_BLOCK:]
    a = jnp.exp2(lb + jnp.concatenate(tails, axis=1))
    if mask is not None:
        a = jnp.where(mask, a, 0.0)
    return a.astype(BF16), carry


def _sb_decode_step(q_ref, bias_ref, uo_ref, gather_ref, spread_ref, k_refs, v_refs, carry_scr,
                    acc_scr, heads):
    pages = len(k_refs)
    q = q_ref[...]
    rows = k_refs[0].shape[0]
    own = (lax.broadcasted_iota(jnp.int32, (pages * heads, rows), 1) % heads
           == lax.broadcasted_iota(jnp.int32, (pages * heads, rows), 0) % heads)
    z_all = jnp.concatenate([_dot_nt(q, r[...].astype(BF16)) for r in k_refs], axis=0)
    pieces = _split_bf16(jnp.where(own, z_all, 0.0), 2)
    zr = _dot(jnp.concatenate(pieces, axis=1), gather_ref[...])
    zr = zr + jnp.concatenate([bias_ref[...] * LOG2E] * pages, axis=0)
    lb = _log2_sigmoid(zr)
    hi, lo = _split_bf16(lb - zr, 2)
    ts = _dot(jnp.concatenate([hi, lo], axis=1), uo_ref[...])
    carry = carry_scr[...]
    tails = []
    for p in range(pages):
        tsp = ts[p * heads:(p + 1) * heads]
        tails.append(tsp[:, :KEY_BLOCK] + carry)
        carry = carry + tsp[:, KEY_BLOCK:]
    carry_scr[...] = carry
    a = jnp.exp2(lb + jnp.concatenate(tails, axis=0)).astype(BF16)
    a_rows = jnp.where(own, _dot(a, spread_ref[...]), 0.0)
    acc = acc_scr[...]
    for p in range(pages):
        acc = acc + _dot(a_rows[p * heads:(p + 1) * heads].astype(BF16), v_refs[p][...].astype(BF16))
    acc_scr[...] = acc


def _sb_attn_kernel(pt_ref, q_ref, k_ref, v_ref, bias_ref, uo_ref, qs_ref, bias_rows_ref,
                    gather_ref, spread_ref, *refs, tq, group, pages, heads):
    k_refs, v_refs = refs[:pages], refs[pages:2 * pages]
    o_ref, os_ref, carry_scr, acc_scr, dcarry_scr, dacc_scr = refs[2 * pages:]
    qi = pl.program_id(2)
    h = pl.program_id(1)
    q = q_ref[...]
    bias = bias_ref[h] * LOG2E
    uo = uo_ref[...]
    n_diag = tq // KEY_BLOCK

    @pl.when(qi == 0)
    def _():
        dcarry_scr[...] = jnp.zeros_like(dcarry_scr)
        dacc_scr[...] = jnp.zeros_like(dacc_scr)

    def visit(first_block, n, newest):
        rows = n * KEY_BLOCK
        start = pl.multiple_of(first_block * KEY_BLOCK, KEY_BLOCK)
        kb = k_ref[pl.ds(start, rows), :]
        vb = v_ref[pl.ds(start, rows), :]
        mask = None
        if newest:
            mask = (lax.broadcasted_iota(jnp.int32, (tq, rows), 1) + (start - qi * tq)
                    < lax.broadcasted_iota(jnp.int32, (tq, rows), 0))
            _sb_decode_step(qs_ref, bias_rows_ref, uo_ref, gather_ref, spread_ref, k_refs, v_refs,
                            dcarry_scr, dacc_scr, heads)
        a, carry = _sb_group(_dot_nt(q, kb) + bias, carry_scr[...], uo, mask)
        carry_scr[...] = carry
        acc_scr[...] += _dot(a, vb)

    carry_scr[...] = jnp.zeros_like(carry_scr)
    acc_scr[...] = jnp.zeros_like(acc_scr)
    n_tot = (qi + 1) * n_diag
    n_groups = n_tot // group
    leftover = lax.rem(n_tot, group)

    @pl.when(n_groups > 0)
    def _():
        visit(n_tot - group, group, True)

    def body(g, _):
        visit(n_tot - (g + 1) * group, group, False)
        return 0

    lax.fori_loop(1, n_groups, body, 0)
    size = group // 2
    while size >= n_diag:
        take = lax.rem(lax.div(leftover, size), 2) == 1
        is_first = jnp.logical_and(n_groups == 0, lax.div(leftover, 2 * size) == 0)
        for newest in (True, False):
            pl.when(jnp.logical_and(take, is_first == newest))(
                functools.partial(visit, lax.rem(leftover, size), size, newest))
        size //= 2
    o_ref[...] = acc_scr[...]

    @pl.when(qi == pl.num_programs(2) - 1)
    def _():
        os_ref[...] = dacc_scr[...]


def _sb_attention(qkv, b_sb, uo, q_s, cache_k, cache_v, page_table, bias_rows, batch, seq, heads,
                  dh, tq, group=8):
    m = qkv.shape[1]
    nq = seq // tq
    nb, n_pages = page_table.shape
    rows = cache_k.shape[1]
    assert nb == batch * heads and n_pages % nq == 0 and rows == KEY_BLOCK * heads
    pages = n_pages // nq
    kernel = functools.partial(_sb_attn_kernel, tq=tq, group=group, pages=pages, heads=heads)
    key_of_row = lax.broadcasted_iota(jnp.int32, (rows, KEY_BLOCK), 0) // heads
    gather1 = jnp.where(key_of_row == lax.broadcasted_iota(jnp.int32, (rows, KEY_BLOCK), 1), 1.0, 0.0)
    gather = jnp.concatenate([gather1] * 2, axis=0).astype(BF16)
    spread = gather1.T.astype(BF16)

    def page_spec(j):
        return pl.BlockSpec(
            (None, rows, dh),
            lambda b, h, i, pt, j=j: (pt[b * heads + h, n_pages - 1 - (i * pages + j)], 0, 0))

    const = lambda a: pl.BlockSpec(a.shape, lambda b, h, i, pt: (0,) * a.ndim)
    grid_spec = pltpu.PrefetchScalarGridSpec(
        num_scalar_prefetch=1,
        grid=(batch, heads, nq),
        in_specs=[pl.BlockSpec((None, tq, dh), lambda b, h, i, pt: (0, b * nq + i, h)),
                  pl.BlockSpec((None, seq, dh), lambda b, h, i, pt: (1, b, h)),
                  pl.BlockSpec((None, seq, dh), lambda b, h, i, pt: (2, b, h)),
                  pl.BlockSpec(memory_space=pltpu.SMEM),
                  const(uo),
                  pl.BlockSpec((None, heads, dh), lambda b, h, i, pt: (b * heads + h, 0, 0)),
                  const(bias_rows), const(gather), const(spread)]
                 + [page_spec(j) for j in range(pages)] * 2,
        out_specs=[pl.BlockSpec((tq, dh), lambda b, h, i, pt: (b * nq + i, h)),
                   pl.BlockSpec((None, heads, dh), lambda b, h, i, pt: (b * heads + h, 0, 0))],
        scratch_shapes=[pltpu.VMEM((tq, KEY_BLOCK), F32), pltpu.VMEM((tq, dh), F32),
                        pltpu.VMEM((heads, KEY_BLOCK), F32), pltpu.VMEM((heads, dh), F32)],
    )
    return pl.pallas_call(
        kernel,
        grid_spec=grid_spec,
        out_shape=[jax.ShapeDtypeStruct((m, heads * dh), F32),
                   jax.ShapeDtypeStruct((nb, heads, dh), F32)],
        compiler_params=_params("arbitrary", "arbitrary", "arbitrary"),
        name="sb_attention",
    )(page_table, qkv, qkv, qkv, b_sb, uo, q_s, bias_rows, gather, spread,
      *([cache_k] * pages), *([cache_v] * pages))


def _gla_prefix_matrix():
    t = lax.broadcasted_iota(jnp.int32, (GLA_CHUNK, GLA_CHUNK), 0)
    s = lax.broadcasted_iota(jnp.int32, (GLA_CHUNK, GLA_CHUNK), 1)
    return jnp.where(s <= t, 1.0, 0.0).astype(BF16)


def _gla_intra(qs, ks, bs, ones_bf16):
    n = len(qs)
    c = GLA_CHUNK
    t_i = lax.broadcasted_iota(jnp.int32, (c, c), 0)
    s_i = lax.broadcasted_iota(jnp.int32, (c, c), 1)
    atts = [jnp.zeros((c, c), F32)] * n
    h = c // 2
    while h >= GLA_SUB:
        th, sh = t_i // h, s_i // h
        pair = ((th - sh - 1) | ((th & 1) ^ 1)) == 0
        for ci in range(n):
            q, k, b = qs[ci], ks[ci], bs[ci]
            ref = jnp.concatenate(
                [jnp.broadcast_to(b[j + h - 1:j + h], (2 * h, b.shape[1])) for j in range(0, c, 2 * h)],
                axis=0)
            ql = (q * jnp.exp(jnp.minimum(b - ref, 0.0))).astype(BF16)
            kl = (k * jnp.exp(jnp.minimum(ref - b, 0.0))).astype(BF16)
            atts[ci] = atts[ci] + jnp.where(pair, _dot_nt(ql, kl), 0.0)
        h //= 2
    prods = []
    for ci in range(n):
        q, k, b = qs[ci], ks[ci], bs[ci]
        for lo_ in range(0, c, GLA_SUB):
            qi, ki, bi = (a[lo_:lo_ + GLA_SUB] for a in (q, k, b))
            prods += [qi * ki[s:s + 1] * jnp.exp(jnp.minimum(bi - bi[s:s + 1], 0.0))
                      for s in range(GLA_SUB)]
    rsum = _dot(jnp.concatenate(prods, axis=0).astype(BF16), ones_bf16)
    lane = lax.broadcasted_iota(jnp.int32, (GLA_SUB, LANES), 1)
    trow = lax.broadcasted_iota(jnp.int32, (GLA_SUB, LANES), 0)
    for ci in range(n):
        strips = []
        for i, lo_ in enumerate(range(0, c, GLA_SUB)):
            base = (ci * (c // GLA_SUB) + i) * GLA_SUB * GLA_SUB
            strip = jnp.zeros((GLA_SUB, LANES), F32)
            for s in range(GLA_SUB):
                strip = jnp.where(lane == lo_ + s,
                                  rsum[base + s * GLA_SUB:base + (s + 1) * GLA_SUB], strip)
            strips.append(jnp.where(lane - lo_ <= trow, strip, 0.0)[:, :c])
        atts[ci] = atts[ci] + jnp.concatenate(strips, axis=0)
    return atts


def _gla_prompt_kernel(q_ref, k_ref, v_ref, gl_ref, wg_ref, bg_ref, ones_ref, ll_ref, o_ref,
                       st_ref, st_scr, *, n_inner, scale):
    @pl.when(pl.program_id(2) == 0)
    def _():
        st_scr[...] = jnp.zeros_like(st_scr)

    c = GLA_CHUNK
    ll = ll_ref[...]
    g = _log_sigmoid(_dot(gl_ref[...].astype(BF16), wg_ref[...]) + bg_ref[...]) * (1.0 / GLA_GATE_NORM)
    chunks = [slice(ci * c, (ci + 1) * c) for ci in range(n_inner)]
    bs = [sum(_dot(ll, piece) for piece in _split_bf16(g[sl], 3)) for sl in chunks]
    qs = [q_ref[sl, :] * scale for sl in chunks]
    ks = [k_ref[sl, :] for sl in chunks]
    atts = _gla_intra(qs, ks, bs, ones_ref[...])
    st = st_scr[...]
    for ci, sl in enumerate(chunks):
        q, k, b = qs[ci], ks[ci], bs[ci]
        v16 = v_ref[sl, :].astype(BF16)
        o_ref[sl, :] = (_dot_nt((q * jnp.exp(b)).astype(BF16), st.astype(BF16))
                        + _dot(atts[ci].astype(BF16), v16))
        b_last = b[c - 1:c]
        kdec = (k * jnp.exp(b_last - b)).astype(BF16)
        st = st * jnp.exp(b_last) + _dot_tn(v16, kdec)
    st_scr[...] = st
    st_ref[...] = st


def _gla_prompt(proj, glow, wgu, b_gate, ones, batch, seq, heads, dk, dv, q_col, k_col, v_col,
                n_inner=32):
    m = proj.shape[0]
    n_inner = min(n_inner, seq // GLA_CHUNK)
    cb = n_inner * GLA_CHUNK
    nc = seq // cb
    kernel = functools.partial(_gla_prompt_kernel, n_inner=n_inner, scale=dk ** -0.5)
    return pl.pallas_call(
        kernel,
        grid=(batch, heads, nc),
        in_specs=[pl.BlockSpec((cb, dk), lambda b, h, c: (b * nc + c, q_col // dk + h)),
                  pl.BlockSpec((cb, dk), lambda b, h, c: (b * nc + c, k_col // dk + h)),
                  pl.BlockSpec((cb, dv), lambda b, h, c: (b * nc + c, v_col // dv + h)),
                  pl.BlockSpec((cb, LANES), lambda b, h, c: (b * nc + c, 0)),
                  pl.BlockSpec((LANES, dk), lambda b, h, c: (0, h)),
                  pl.BlockSpec((1, dk), lambda b, h, c: (0, h)),
                  pl.BlockSpec((LANES, LANES), lambda b, h, c: (0, 0)),
                  pl.BlockSpec((GLA_CHUNK, GLA_CHUNK), lambda b, h, c: (0, 0))],
        out_specs=[pl.BlockSpec((cb, dv), lambda b, h, c: (b * nc + c, h)),
                   pl.BlockSpec((None, None, dv, dk), lambda b, h, c: (b, h, 0, 0))],
        out_shape=[jax.ShapeDtypeStruct((m, heads * dv), F32),
                   jax.ShapeDtypeStruct((batch, heads, dv, dk), F32)],
        scratch_shapes=[pltpu.VMEM((dv, dk), F32)],
        compiler_params=_params("arbitrary", "arbitrary", "arbitrary"),
        name="gla_prompt",
    )(proj, proj, proj, glow, wgu, b_gate, ones, _gla_prefix_matrix())


def _gla_step_kernel(q_ref, k_ref, v_ref, gl_ref, wg_ref, bg_ref, s_ref, o_ref, so_ref, *,
                     heads, dk, dv, scale):
    eye = (lax.broadcasted_iota(jnp.int32, (dk, dk), 0)
           == lax.broadcasted_iota(jnp.int32, (dk, dk), 1))

    def column(x_row):
        return jnp.sum(jnp.where(eye, x_row, 0.0), axis=1, keepdims=True)

    for i in range(q_ref.shape[0]):
        g = _log_sigmoid(_dot(gl_ref[i].astype(BF16), wg_ref[...]) + bg_ref[...]) * (1.0 / GLA_GATE_NORM)
        q = q_ref[i] * scale
        k = k_ref[i]
        v = v_ref[i]
        for h in range(heads):
            ksl = slice(h * dk, (h + 1) * dk)
            vsl = slice(h * dv, (h + 1) * dv)
            s_new = column(jnp.exp(g[:, ksl])) * s_ref[i, h] + column(k[:, ksl]) * v[:, vsl]
            so_ref[i, h] = s_new
            o_ref[i, :, vsl] = jnp.sum(column(q[:, ksl]) * s_new, axis=0, keepdims=True)


def _gla_step(proj, glow, wgu, b_gate, state, dk, dv, q_col, k_col, v_col, seqs=4):
    nb, heads = state.shape[0], state.shape[1]
    kw, vw = heads * dk, heads * dv
    kernel = functools.partial(_gla_step_kernel, heads=heads, dk=dk, dv=dv, scale=dk ** -0.5)
    proj3 = proj.reshape(nb, 1, proj.shape[1])
    glow3 = glow.reshape(nb, 1, LANES)
    o, s_new = pl.pallas_call(
        kernel,
        grid=(nb // seqs,),
        in_specs=[pl.BlockSpec((seqs, 1, kw), lambda b: (b, 0, q_col // kw)),
                  pl.BlockSpec((seqs, 1, kw), lambda b: (b, 0, k_col // kw)),
                  pl.BlockSpec((seqs, 1, vw), lambda b: (b, 0, v_col // vw)),
                  pl.BlockSpec((seqs, 1, LANES), lambda b: (b, 0, 0)),
                  pl.BlockSpec((LANES, kw), lambda b: (0, 0)),
                  pl.BlockSpec((1, kw), lambda b: (0, 0)),
                  pl.BlockSpec((seqs, heads, dk, dv), lambda b: (b, 0, 0, 0))],
        out_specs=[pl.BlockSpec((seqs, 1, vw), lambda b: (b, 0, 0)),
                   pl.BlockSpec((seqs, heads, dk, dv), lambda b: (b, 0, 0, 0))],
        out_shape=[jax.ShapeDtypeStruct((nb, 1, vw), F32),
                   jax.ShapeDtypeStruct(state.shape, F32)],
        compiler_params=_params("arbitrary"),
        name="gla_step",
    )(proj3, proj3, proj3, glow3, wgu, b_gate, state)
    return o.reshape(nb, vw), s_new


def _out_proj_kernel(osb_ref, og_ref, rg_ref, x_ref, gsb_ref, ggla_ref, w_ref, gpost_ref,
                     gt_ref, gpre_ref, sc_ref, sh_ref, x1_ref, h2_ref, mix_scr, *, dh, dv):
    sbw = osb_ref.shape[1]
    for h in range(sbw // dh):
        sl = slice(h * dh, (h + 1) * dh)
        mix_scr[:, sl] = _rms(osb_ref[:, sl], gsb_ref[...]).astype(BF16)
    for h in range(og_ref.shape[1] // dv):
        sl = slice(h * dv, (h + 1) * dv)
        y = _rms(og_ref[:, sl], ggla_ref[...]) * _silu(rg_ref[:, sl])
        mix_scr[:, sbw + h * dv:sbw + (h + 1) * dv] = y.astype(BF16)
    m = _dot(mix_scr[...], w_ref[...])
    x1 = x_ref[...] + gt_ref[...] * _rms(m, gpost_ref[...])
    x1_ref[...] = x1
    h2 = _rms(x1, gpre_ref[...]) * (1.0 + sc_ref[...]) + sh_ref[...]
    h2_ref[...] = h2.astype(BF16)


def _out_proj(osb, og, proj, rg_col, x, g_sb, g_gla, w, g_post, gt, g_pre, sc, sh, tm,
              rows_per_group):
    m, d = x.shape
    sbw, gw = osb.shape[1], og.shape[1]
    dh, dv = g_sb.shape[1], g_gla.shape[1]
    r = gt.shape[1]
    mod_spec = pl.BlockSpec((None, r, d), lambda i: ((i * tm) // rows_per_group, 0, 0))
    row = lambda width: pl.BlockSpec((1, width), lambda i: (0, 0))
    kernel = functools.partial(_out_proj_kernel, dh=dh, dv=dv)
    return pl.pallas_call(
        kernel,
        grid=(m // tm,),
        in_specs=[pl.BlockSpec((tm, sbw), lambda i: (i, 0)),
                  pl.BlockSpec((tm, gw), lambda i: (i, 0)),
                  pl.BlockSpec((tm, gw), lambda i: (i, rg_col // gw)),
                  pl.BlockSpec((tm, d), lambda i: (i, 0)),
                  row(dh), row(dv),
                  pl.BlockSpec((sbw + gw, d), lambda i: (0, 0)),
                  row(d), mod_spec, row(d), mod_spec, mod_spec],
        out_specs=[pl.BlockSpec((tm, d), lambda i: (i, 0)),
                   pl.BlockSpec((tm, d), lambda i: (i, 0))],
        out_shape=[jax.ShapeDtypeStruct((m, d), F32),
                   jax.ShapeDtypeStruct((m, d), BF16)],
        scratch_shapes=[pltpu.VMEM((tm, sbw + gw), BF16)],
        compiler_params=_params("arbitrary"),
        name="out_proj",
    )(osb, og, proj, x, g_sb, g_gla, w, g_post, gt, g_pre, sc, sh)


def _ffn_kernel(h_ref, wg_ref, wu_ref, wd_ref, x1_ref, gpost_ref, gt_ref, y_ref, acc_scr):
    j = pl.program_id(1)
    last = pl.num_programs(1) - 1

    def part():
        h = h_ref[...]
        t = _silu(_dot(h, wg_ref[...])) * _dot(h, wu_ref[...])
        return _dot(t.astype(BF16), wd_ref[...])

    @pl.when(j == 0)
    def _():
        acc_scr[...] = part()

    @pl.when(jnp.logical_and(j > 0, j < last))
    def _():
        acc_scr[...] += part()

    @pl.when(j == last)
    def _():
        y_ref[...] = x1_ref[...] + gt_ref[...] * _rms(acc_scr[...] + part(), gpost_ref[...])


def _ffn(h2, wg, wu, wd, x1, g_post, gt, tm, rows_per_group, tf=512):
    m, d = x1.shape
    f = wg.shape[1]
    r = gt.shape[1]
    assert f // tf >= 2
    return pl.pallas_call(
        _ffn_kernel,
        grid=(m // tm, f // tf),
        in_specs=[pl.BlockSpec((tm, d), lambda i, j: (i, 0)),
                  pl.BlockSpec((d, tf), lambda i, j: (0, j)),
                  pl.BlockSpec((d, tf), lambda i, j: (0, j)),
                  pl.BlockSpec((tf, d), lambda i, j: (j, 0)),
                  pl.BlockSpec((tm, d), lambda i, j: (i, 0)),
                  pl.BlockSpec((1, d), lambda i, j: (0, 0)),
                  pl.BlockSpec((None, r, d), lambda i, j: ((i * tm) // rows_per_group, 0, 0))],
        out_specs=pl.BlockSpec((tm, d), lambda i, j: (i, 0)),
        out_shape=jax.ShapeDtypeStruct((m, d), F32),
        scratch_shapes=[pltpu.VMEM((tm, d), F32)],
        compiler_params=_params("arbitrary", "arbitrary"),
        name="ffn",
    )(h2, wg, wu, wd, x1, g_post, gt)


def _suffix_sum_matrix():
    j = lax.broadcasted_iota(jnp.int32, (2 * KEY_BLOCK, 2 * KEY_BLOCK), 0) % KEY_BLOCK
    s = lax.broadcasted_iota(jnp.int32, (2 * KEY_BLOCK, 2 * KEY_BLOCK), 1)
    return jnp.where((j > s) | (s >= KEY_BLOCK), 1.0, 0.0).astype(BF16)


def kernel(x_prompt, x_sample, c_prompt, c_sample, cache_k, cache_v, page_table, state_gla, w_ada, b_ada, g_pre_mix, w_in, b_sb, w_gate_up, b_gate, g_sb_out, g_gla_out, w_out, g_post_mix, g_pre_ffn, w_ffn_gate, w_ffn_up, w_ffn_down, g_post_ffn):
    batch, seq, d = x_prompt.shape
    nb = x_sample.shape[0]
    depth, n_pool, page, sb_heads, dh = cache_k.shape
    gla_heads, dk, dv = state_gla.shape[2:]
    rank = w_gate_up.shape[1]
    assert depth == 1 and x_sample.shape[1] == 1
    sbw, kw, vw = sb_heads * dh, gla_heads * dk, gla_heads * dv
    assert 2 * kw == sbw and vw == sbw
    q_col, k_col, v_col, rg_col = 0, kw, sbw, 2 * sbw
    main = 3 * sbw + 2 * kw + 2 * vw

    row = lambda a: a.reshape(1, -1)
    uo = _suffix_sum_matrix()
    ones = jnp.ones((LANES, LANES), BF16)

    w_in0 = w_in[0]
    w_main = w_in0.astype(BF16)
    w_low = jnp.pad(w_in0[:, main:], ((0, 0), (0, LANES - rank))).astype(BF16)
    wgu = jnp.pad(w_gate_up[0], ((0, LANES - rank), (0, 0))).astype(BF16)
    w_o = w_out[0].astype(BF16)
    w_fg, w_fu, w_fd = (w[0].astype(BF16) for w in (w_ffn_gate, w_ffn_up, w_ffn_down))

    n_c = batch + nb
    c_all = jnp.pad(jnp.concatenate([c_prompt, c_sample], axis=0), ((0, -n_c % SUBLANES), (0, 0)))
    mod = _ada(c_all, w_ada[0], row(b_ada[0]))
    mod_p = [a.reshape(batch, 1, d) for a in jnp.split(mod[:batch], 6, axis=-1)]
    mod_s = [a.reshape(1, nb, d) for a in jnp.split(mod[batch:n_c], 6, axis=-1)]

    gains = dict(g_sb=row(g_sb_out[0]), g_gla=row(g_gla_out[0]), g_post=row(g_post_mix[0]),
                 g_pre=row(g_pre_ffn[0]))

    def rest_of_layer(x2, osb, og, proj, mods, tm, rows_per_group, ffn_tm):
        sh1, sc1, gt1, sh2, sc2, gt2 = mods
        x1, h2 = _out_proj(osb, og, proj, rg_col, x2, gains["g_sb"], gains["g_gla"], w_o,
                           gains["g_post"], gt1, gains["g_pre"], sc2, sh2, tm, rows_per_group)
        return _ffn(h2, w_fg, w_fu, w_fd, x1, row(g_post_ffn[0]), gt2, ffn_tm, rows_per_group)

    xp = x_prompt.reshape(batch * seq, d)
    xs = x_sample.reshape(nb, d)
    qkv_p, k_p, v_p, gla_p, glow_p = _in_proj(xp, row(g_pre_mix[0]), mod_p[1], mod_p[0], w_main, w_low,
                                              1024, seq, sb_heads, dh)
    qkv_s, k_s, v_s, gla_s, glow_s = _in_proj(xs, row(g_pre_mix[0]), mod_s[1], mod_s[0], w_main, w_low,
                                              nb, nb, sb_heads, dh)

    bias_rows = jnp.broadcast_to(b_sb[0][:, None], (sb_heads, KEY_BLOCK))
    osb_p, osb_s = _sb_attention(qkv_p, b_sb[0], uo, qkv_s[0].reshape(nb, sb_heads, dh),
                                 cache_k[0].reshape(n_pool, page * sb_heads, dh),
                                 cache_v[0].reshape(n_pool, page * sb_heads, dh),
                                 page_table, bias_rows, batch, seq, sb_heads, dh, tq=512)

    og_p, st_p = _gla_prompt(gla_p, glow_p, wgu, row(b_gate[0]), ones, batch, seq, gla_heads,
                             dk, dv, q_col, k_col, v_col)
    y_p = rest_of_layer(xp, osb_p, og_p, gla_p, mod_p, 256, seq, 512)
    og_s, st_s = _gla_step(gla_s, glow_s, wgu, row(b_gate[0]), state_gla[0], dk, dv,
                           q_col, k_col, v_col)
    y_s = rest_of_layer(xs, osb_s.reshape(nb, sbw), og_s, gla_s, mod_s, nb, nb, nb)

    kv = lambda a, n: a.reshape(1, n, -1, sb_heads, dh)
    return (y_p.reshape(batch, seq, d), y_s.reshape(nb, 1, d),
            kv(k_p, batch), kv(v_p, batch),
            jnp.swapaxes(st_p, -1, -2)[None],
            kv(k_s, nb), kv(v_s, nb), st_s[None])
```

```python
import functools

import jax
import jax.numpy as jnp
from jax import lax
from jax.experimental import pallas as pl
from jax.experimental.pallas import tpu as pltpu

F32 = jnp.float32
BF16 = jnp.bfloat16

LANES = 128
SUBLANES = 8
VMEM_LIMIT_BYTES = 52 * 1024 * 1024

EPS = 1e-6
LOG2E = 1.4426950408889634
GLA_GATE_NORM = 16.0
GLA_CHUNK = 64
GLA_SUB = 8
KEY_BLOCK = 128

ADA_COL_TILE = 1024
IN_PROJ_ROW_TILE = 1024
ATTN_QUERY_TILE = 512
ATTN_KEY_GROUP = 8
GLA_CHUNKS_PER_STEP = 32
GLA_STEP_SEQS = 4
OUT_PROJ_ROW_TILE = 256
FFN_ROW_TILE = 512
FFN_COL_TILE = 512


def _params(*sem):
    return pltpu.CompilerParams(dimension_semantics=sem, vmem_limit_bytes=VMEM_LIMIT_BYTES)


def _log_sigmoid(z):
    return jnp.minimum(z, 0.0) - jnp.log(1.0 + jnp.exp(-jnp.abs(z)))


def _log2_sigmoid(z2):
    return jnp.minimum(z2, 0.0) - jnp.log(1.0 + jnp.exp2(-jnp.abs(z2))) * LOG2E


def _silu(x):
    return x * (1.0 / (1.0 + jnp.exp(-x)))


def _rms(x, g):
    ms = jnp.mean(x * x, axis=-1, keepdims=True)
    return x * lax.rsqrt(ms + EPS) * g


def _split_bf16(x, pieces):
    out = []
    for _ in range(pieces - 1):
        p = x.astype(BF16)
        out.append(p)
        x = x - p.astype(F32)
    out.append(x.astype(BF16))
    return out


def _dot(a, b):
    return jnp.dot(a, b, preferred_element_type=F32)


def _dot_nt(a, b):
    return lax.dot_general(a, b, (((1,), (1,)), ((), ())), preferred_element_type=F32)


def _dot_tn(a, b):
    return lax.dot_general(a, b, (((0,), (0,)), ((), ())), preferred_element_type=F32)


def _ada_kernel(c_ref, w_ref, b_ref, o_ref):
    a = _silu(c_ref[...]).astype(BF16)
    o_ref[...] = _dot(a, w_ref[...].astype(BF16)) + b_ref[...]


def _ada(c, w, b, tn=ADA_COL_TILE):
    m, d = c.shape
    n = w.shape[1]
    return pl.pallas_call(
        _ada_kernel,
        grid=(n // tn,),
        in_specs=[pl.BlockSpec((m, d), lambda j: (0, 0)),
                  pl.BlockSpec((d, tn), lambda j: (0, j)),
                  pl.BlockSpec((1, tn), lambda j: (0, j))],
        out_specs=pl.BlockSpec((m, tn), lambda j: (0, j)),
        out_shape=jax.ShapeDtypeStruct((m, n), F32),
        compiler_params=_params("arbitrary"),
        name="ada_mod",
    )(c, w, b)


def _prenorm(x_ref, g_ref, sc_ref, sh_ref, h_scr):
    h = _rms(x_ref[...], g_ref[...]) * (1.0 + sc_ref[...]) + sh_ref[...]
    h_scr[...] = h.astype(BF16)


def _in_proj_attn_kernel(x_ref, g_ref, sc_ref, sh_ref, w_ref, qkv_ref, krow_ref, vrow_ref, h_scr,
                         *, heads, dh, q_scale):
    j = pl.program_id(1)

    def product():
        return _dot(h_scr[...], w_ref[...])

    @pl.when(j == 0)
    def _():
        _prenorm(x_ref, g_ref, sc_ref, sh_ref, h_scr)
        qkv_ref[...] = (product() * q_scale).astype(BF16)

    for tile, rows_ref in ((1, krow_ref), (2, vrow_ref)):
        @pl.when(j == tile)
        def _(rows_ref=rows_ref):
            res = product()
            qkv_ref[...] = res.astype(BF16)
            for h in range(heads):
                rows_ref[pl.ds(h, res.shape[0], stride=heads), :] = res[:, h * dh:(h + 1) * dh]


def _in_proj_gla_kernel(x_ref, g_ref, sc_ref, sh_ref, w_ref, wl_ref, gla_ref, ol_ref, h_scr):
    @pl.when(pl.program_id(1) == 0)
    def _():
        _prenorm(x_ref, g_ref, sc_ref, sh_ref, h_scr)
        ol_ref[...] = _dot(h_scr[...], wl_ref[...])

    gla_ref[...] = _dot(h_scr[...], w_ref[...])


def _in_proj(x, g, sc, sh, w, wl, tm, rows_per_group, heads, dh):
    m, d = x.shape
    tn = heads * dh
    r = sc.shape[1]
    mod_spec = pl.BlockSpec((None, r, d), lambda i, j: ((i * tm) // rows_per_group, 0, 0))
    common = [pl.BlockSpec((tm, d), lambda i, j: (i, 0)),
              pl.BlockSpec((1, d), lambda i, j: (0, 0)),
              mod_spec, mod_spec]
    attn_kernel = functools.partial(_in_proj_attn_kernel, heads=heads, dh=dh,
                                    q_scale=dh ** -0.5 * LOG2E)
    qkv, krow, vrow = pl.pallas_call(
        attn_kernel,
        grid=(m // tm, 3),
        in_specs=common + [pl.BlockSpec((d, tn), lambda i, j: (0, j))],
        out_specs=[pl.BlockSpec((None, tm, tn), lambda i, j: (j, i, 0)),
                   pl.BlockSpec((tm * heads, dh), lambda i, j: (i, 0)),
                   pl.BlockSpec((tm * heads, dh), lambda i, j: (i, 0))],
        out_shape=[jax.ShapeDtypeStruct((3, m, tn), BF16),
                   jax.ShapeDtypeStruct((m * heads, dh), F32),
                   jax.ShapeDtypeStruct((m * heads, dh), F32)],
        scratch_shapes=[pltpu.VMEM((tm, d), BF16)],
        compiler_params=_params("arbitrary", "arbitrary"),
        name="in_proj_attn",
    )(x, g, sc, sh, w)
    gla, glow = pl.pallas_call(
        _in_proj_gla_kernel,
        grid=(m // tm, 3),
        in_specs=common + [pl.BlockSpec((d, tn), lambda i, j: (0, 3 + j)),
                           pl.BlockSpec((d, LANES), lambda i, j: (0, 0))],
        out_specs=[pl.BlockSpec((tm, tn), lambda i, j: (i, j)),
                   pl.BlockSpec((tm, LANES), lambda i, j: (i, 0))],
        out_shape=[jax.ShapeDtypeStruct((m, 3 * tn), F32),
                   jax.ShapeDtypeStruct((m, LANES), F32)],
        scratch_shapes=[pltpu.VMEM((tm, d), BF16)],
        compiler_params=_params("arbitrary", "arbitrary"),
        name="in_proj_gla",
    )(x, g, sc, sh, w, wl)
    return qkv, krow, vrow, gla, glow


def _sb_group(z, carry, uo, mask):
    n = z.shape[1] // KEY_BLOCK
    lb = _log2_sigmoid(z)
    lk = lb - z
    if mask is not None:
        lk = jnp.where(mask, lk, 0.0)
    hi, lo = _split_bf16(lk, 2)
    tails = [None] * n
    for s in reversed(range(n)):
        sl = slice(s * KEY_BLOCK, (s + 1) * KEY_BLOCK)
        ts = _dot(jnp.concatenate([hi[:, sl], lo[:, sl]], axis=1), uo)
        tails[s] = ts[:, :KEY_BLOCK] + carry
        carry = carry + ts[:, KEY_BLOCK:]
    a = jnp.exp2(lb + jnp.concatenate(tails, axis=1))
    if mask is not None:
        a = jnp.where(mask, a, 0.0)
    return a.astype(BF16), carry


def _sb_decode_step(q_ref, bias_ref, uo_ref, gather_ref, spread_ref, k_refs, v_refs, carry_scr,
                    acc_scr, heads):
    pages = len(k_refs)
    q = q_ref[...]
    rows = k_refs[0].shape[0]
    own = (lax.broadcasted_iota(jnp.int32, (pages * heads, rows), 1) % heads
           == lax.broadcasted_iota(jnp.int32, (pages * heads, rows), 0) % heads)
    z_all = jnp.concatenate([_dot_nt(q, r[...].astype(BF16)) for r in k_refs], axis=0)
    pieces = _split_bf16(jnp.where(own, z_all, 0.0), 2)
    zr = _dot(jnp.concatenate(pieces, axis=1), gather_ref[...])
    zr = zr + jnp.concatenate([bias_ref[...] * LOG2E] * pages, axis=0)
    lb = _log2_sigmoid(zr)
    hi, lo = _split_bf16(lb - zr, 2)
    ts = _dot(jnp.concatenate([hi, lo], axis=1), uo_ref[...])
    carry = carry_scr[...]
    tails = []
    for p in range(pages):
        tsp = ts[p * heads:(p + 1) * heads]
        tails.append(tsp[:, :KEY_BLOCK] + carry)
        carry = carry + tsp[:, KEY_BLOCK:]
    carry_scr[...] = carry
    a = jnp.exp2(lb + jnp.concatenate(tails, axis=0)).astype(BF16)
    a_rows = jnp.where(own, _dot(a, spread_ref[...]), 0.0)
    acc = acc_scr[...]
    for p in range(pages):
        acc = acc + _dot(a_rows[p * heads:(p + 1) * heads].astype(BF16), v_refs[p][...].astype(BF16))
    acc_scr[...] = acc


def _sb_attn_kernel(pt_ref, q_ref, k_ref, v_ref, bias_ref, uo_ref, qs_ref, bias_rows_ref,
                    gather_ref, spread_ref, *refs, tq, group, pages, heads):
    k_refs, v_refs = refs[:pages], refs[pages:2 * pages]
    o_ref, os_ref, carry_scr, acc_scr, dcarry_scr, dacc_scr = refs[2 * pages:]
    qi = pl.program_id(2)
    h = pl.program_id(1)
    q = q_ref[...]
    bias = bias_ref[h] * LOG2E
    uo = uo_ref[...]
    n_diag = tq // KEY_BLOCK

    @pl.when(qi == 0)
    def _():
        dcarry_scr[...] = jnp.zeros_like(dcarry_scr)
        dacc_scr[...] = jnp.zeros_like(dacc_scr)

    def visit(first_block, n, newest):
        rows = n * KEY_BLOCK
        start = pl.multiple_of(first_block * KEY_BLOCK, KEY_BLOCK)
        kb = k_ref[pl.ds(start, rows), :]
        vb = v_ref[pl.ds(start, rows), :]
        mask = None
        if newest:
            mask = (lax.broadcasted_iota(jnp.int32, (tq, rows), 1) + (start - qi * tq)
                    < lax.broadcasted_iota(jnp.int32, (tq, rows), 0))
            _sb_decode_step(qs_ref, bias_rows_ref, uo_ref, gather_ref, spread_ref, k_refs, v_refs,
                            dcarry_scr, dacc_scr, heads)
        a, carry = _sb_group(_dot_nt(q, kb) + bias, carry_scr[...], uo, mask)
        carry_scr[...] = carry
        acc_scr[...] += _dot(a, vb)

    carry_scr[...] = jnp.zeros_like(carry_scr)
    acc_scr[...] = jnp.zeros_like(acc_scr)
    n_tot = (qi + 1) * n_diag
    n_groups = n_tot // group
    leftover = lax.rem(n_tot, group)

    @pl.when(n_groups > 0)
    def _():
        visit(n_tot - group, group, True)

    def body(g, _):
        visit(n_tot - (g + 1) * group, group, False)
        return 0

    lax.fori_loop(1, n_groups, body, 0)
    size = group // 2
    while size >= n_diag:
        take = lax.rem(lax.div(leftover, size), 2) == 1
        is_first = jnp.logical_and(n_groups == 0, lax.div(leftover, 2 * size) == 0)
        for newest in (True, False):
            pl.when(jnp.logical_and(take, is_first == newest))(
                functools.partial(visit, lax.rem(leftover, size), size, newest))
        size //= 2
    o_ref[...] = acc_scr[...]

    @pl.when(qi == pl.num_programs(2) - 1)
    def _():
        os_ref[...] = dacc_scr[...]


def _sb_attention(qkv, b_sb, uo, q_s, cache_k, cache_v, page_table, bias_rows, batch, seq, heads,
                  dh, tq, group=ATTN_KEY_GROUP):
    m = qkv.shape[1]
    nq = seq // tq
    nb, n_pages = page_table.shape
    rows = cache_k.shape[1]
    assert nb == batch * heads and n_pages % nq == 0 and rows == KEY_BLOCK * heads
    pages = n_pages // nq
    kernel = functools.partial(_sb_attn_kernel, tq=tq, group=group, pages=pages, heads=heads)
    key_of_row = lax.broadcasted_iota(jnp.int32, (rows, KEY_BLOCK), 0) // heads
    gather1 = jnp.where(key_of_row == lax.broadcasted_iota(jnp.int32, (rows, KEY_BLOCK), 1), 1.0, 0.0)
    gather = jnp.concatenate([gather1] * 2, axis=0).astype(BF16)
    spread = gather1.T.astype(BF16)

    def page_spec(j):
        return pl.BlockSpec(
            (None, rows, dh),
            lambda b, h, i, pt, j=j: (pt[b * heads + h, n_pages - 1 - (i * pages + j)], 0, 0))

    const = lambda a: pl.BlockSpec(a.shape, lambda b, h, i, pt: (0,) * a.ndim)
    grid_spec = pltpu.PrefetchScalarGridSpec(
        num_scalar_prefetch=1,
        grid=(batch, heads, nq),
        in_specs=[pl.BlockSpec((None, tq, dh), lambda b, h, i, pt: (0, b * nq + i, h)),
                  pl.BlockSpec((None, seq, dh), lambda b, h, i, pt: (1, b, h)),
                  pl.BlockSpec((None, seq, dh), lambda b, h, i, pt: (2, b, h)),
                  pl.BlockSpec(memory_space=pltpu.SMEM),
                  const(uo),
                  pl.BlockSpec((None, heads, dh), lambda b, h, i, pt: (b * heads + h, 0, 0)),
                  const(bias_rows), const(gather), const(spread)]
                 + [page_spec(j) for j in range(pages)] * 2,
        out_specs=[pl.BlockSpec((tq, dh), lambda b, h, i, pt: (b * nq + i, h)),
                   pl.BlockSpec((None, heads, dh), lambda b, h, i, pt: (b * heads + h, 0, 0))],
        scratch_shapes=[pltpu.VMEM((tq, KEY_BLOCK), F32), pltpu.VMEM((tq, dh), F32),
                        pltpu.VMEM((heads, KEY_BLOCK), F32), pltpu.VMEM((heads, dh), F32)],
    )
    return pl.pallas_call(
        kernel,
        grid_spec=grid_spec,
        out_shape=[jax.ShapeDtypeStruct((m, heads * dh), F32),
                   jax.ShapeDtypeStruct((nb, heads, dh), F32)],
        compiler_params=_params("arbitrary", "arbitrary", "arbitrary"),
        name="sb_attention",
    )(page_table, qkv, qkv, qkv, b_sb, uo, q_s, bias_rows, gather, spread,
      *([cache_k] * pages), *([cache_v] * pages))


def _gla_prefix_matrix():
    t = lax.broadcasted_iota(jnp.int32, (GLA_CHUNK, GLA_CHUNK), 0)
    s = lax.broadcasted_iota(jnp.int32, (GLA_CHUNK, GLA_CHUNK), 1)
    return jnp.where(s <= t, 1.0, 0.0).astype(BF16)


def _gla_intra(qs, ks, bs, ones_bf16):
    n = len(qs)
    c = GLA_CHUNK
    t_i = lax.broadcasted_iota(jnp.int32, (c, c), 0)
    s_i = lax.broadcasted_iota(jnp.int32, (c, c), 1)
    atts = [jnp.zeros((c, c), F32)] * n
    h = c // 2
    while h >= GLA_SUB:
        th, sh = t_i // h, s_i // h
        pair = ((th - sh - 1) | ((th & 1) ^ 1)) == 0
        for ci in range(n):
            q, k, b = qs[ci], ks[ci], bs[ci]
            ref = jnp.concatenate(
                [jnp.broadcast_to(b[j + h - 1:j + h], (2 * h, b.shape[1])) for j in range(0, c, 2 * h)],
                axis=0)
            ql = (q * jnp.exp(jnp.minimum(b - ref, 0.0))).astype(BF16)
            kl = (k * jnp.exp(jnp.minimum(ref - b, 0.0))).astype(BF16)
            atts[ci] = atts[ci] + jnp.where(pair, _dot_nt(ql, kl), 0.0)
        h //= 2
    prods = []
    for ci in range(n):
        q, k, b = qs[ci], ks[ci], bs[ci]
        for lo_ in range(0, c, GLA_SUB):
            qi, ki, bi = (a[lo_:lo_ + GLA_SUB] for a in (q, k, b))
            prods += [qi * ki[s:s + 1] * jnp.exp(jnp.minimum(bi - bi[s:s + 1], 0.0))
                      for s in range(GLA_SUB)]
    rsum = _dot(jnp.concatenate(prods, axis=0).astype(BF16), ones_bf16)
    lane = lax.broadcasted_iota(jnp.int32, (GLA_SUB, LANES), 1)
    trow = lax.broadcasted_iota(jnp.int32, (GLA_SUB, LANES), 0)
    for ci in range(n):
        strips = []
        for i, lo_ in enumerate(range(0, c, GLA_SUB)):
            base = (ci * (c // GLA_SUB) + i) * GLA_SUB * GLA_SUB
            strip = jnp.zeros((GLA_SUB, LANES), F32)
            for s in range(GLA_SUB):
                strip = jnp.where(lane == lo_ + s,
                                  rsum[base + s * GLA_SUB:base + (s + 1) * GLA_SUB], strip)
            strips.append(jnp.where(lane - lo_ <= trow, strip, 0.0)[:, :c])
        atts[ci] = atts[ci] + jnp.concatenate(strips, axis=0)
    return atts


def _gla_prompt_kernel(q_ref, k_ref, v_ref, gl_ref, wg_ref, bg_ref, ones_ref, ll_ref, o_ref,
                       st_ref, st_scr, *, n_inner, scale):
    @pl.when(pl.program_id(2) == 0)
    def _():
        st_scr[...] = jnp.zeros_like(st_scr)

    c = GLA_CHUNK
    ll = ll_ref[...]
    g = _log_sigmoid(_dot(gl_ref[...].astype(BF16), wg_ref[...]) + bg_ref[...]) * (1.0 / GLA_GATE_NORM)
    chunks = [slice(ci * c, (ci + 1) * c) for ci in range(n_inner)]
    bs = [sum(_dot(ll, piece) for piece in _split_bf16(g[sl], 3)) for sl in chunks]
    qs = [q_ref[sl, :] * scale for sl in chunks]
    ks = [k_ref[sl, :] for sl in chunks]
    atts = _gla_intra(qs, ks, bs, ones_ref[...])
    st = st_scr[...]
    for ci, sl in enumerate(chunks):
        q, k, b = qs[ci], ks[ci], bs[ci]
        v16 = v_ref[sl, :].astype(BF16)
        o_ref[sl, :] = (_dot_nt((q * jnp.exp(b)).astype(BF16), st.astype(BF16))
                        + _dot(atts[ci].astype(BF16), v16))
        b_last = b[c - 1:c]
        kdec = (k * jnp.exp(b_last - b)).astype(BF16)
        st = st * jnp.exp(b_last) + _dot_tn(v16, kdec)
    st_scr[...] = st
    st_ref[...] = st


def _gla_prompt(proj, glow, wgu, b_gate, ones, batch, seq, heads, dk, dv, q_col, k_col, v_col,
                n_inner=GLA_CHUNKS_PER_STEP):
    m = proj.shape[0]
    n_inner = min(n_inner, seq // GLA_CHUNK)
    cb = n_inner * GLA_CHUNK
    nc = seq // cb
    kernel = functools.partial(_gla_prompt_kernel, n_inner=n_inner, scale=dk ** -0.5)
    return pl.pallas_call(
        kernel,
        grid=(batch, heads, nc),
        in_specs=[pl.BlockSpec((cb, dk), lambda b, h, c: (b * nc + c, q_col // dk + h)),
                  pl.BlockSpec((cb, dk), lambda b, h, c: (b * nc + c, k_col // dk + h)),
                  pl.BlockSpec((cb, dv), lambda b, h, c: (b * nc + c, v_col // dv + h)),
                  pl.BlockSpec((cb, LANES), lambda b, h, c: (b * nc + c, 0)),
                  pl.BlockSpec((LANES, dk), lambda b, h, c: (0, h)),
                  pl.BlockSpec((1, dk), lambda b, h, c: (0, h)),
                  pl.BlockSpec((LANES, LANES), lambda b, h, c: (0, 0)),
                  pl.BlockSpec((GLA_CHUNK, GLA_CHUNK), lambda b, h, c: (0, 0))],
        out_specs=[pl.BlockSpec((cb, dv), lambda b, h, c: (b * nc + c, h)),
                   pl.BlockSpec((None, None, dv, dk), lambda b, h, c: (b, h, 0, 0))],
        out_shape=[jax.ShapeDtypeStruct((m, heads * dv), F32),
                   jax.ShapeDtypeStruct((batch, heads, dv, dk), F32)],
        scratch_shapes=[pltpu.VMEM((dv, dk), F32)],
        compiler_params=_params("arbitrary", "arbitrary", "arbitrary"),
        name="gla_prompt",
    )(proj, proj, proj, glow, wgu, b_gate, ones, _gla_prefix_matrix())


def _gla_step_kernel(q_ref, k_ref, v_ref, gl_ref, wg_ref, bg_ref, s_ref, o_ref, so_ref, *,
                     heads, dk, dv, scale):
    eye = (lax.broadcasted_iota(jnp.int32, (dk, dk), 0)
           == lax.broadcasted_iota(jnp.int32, (dk, dk), 1))

    def column(x_row):
        return jnp.sum(jnp.where(eye, x_row, 0.0), axis=1, keepdims=True)

    for i in range(q_ref.shape[0]):
        g = _log_sigmoid(_dot(gl_ref[i].astype(BF16), wg_ref[...]) + bg_ref[...]) * (1.0 / GLA_GATE_NORM)
        q = q_ref[i] * scale
        k = k_ref[i]
        v = v_ref[i]
        for h in range(heads):
            ksl = slice(h * dk, (h + 1) * dk)
            vsl = slice(h * dv, (h + 1) * dv)
            s_new = column(jnp.exp(g[:, ksl])) * s_ref[i, h] + column(k[:, ksl]) * v[:, vsl]
            so_ref[i, h] = s_new
            o_ref[i, :, vsl] = jnp.sum(column(q[:, ksl]) * s_new, axis=0, keepdims=True)


def _gla_step(proj, glow, wgu, b_gate, state, dk, dv, q_col, k_col, v_col, seqs=GLA_STEP_SEQS):
    nb, heads = state.shape[0], state.shape[1]
    kw, vw = heads * dk, heads * dv
    kernel = functools.partial(_gla_step_kernel, heads=heads, dk=dk, dv=dv, scale=dk ** -0.5)
    proj3 = proj.reshape(nb, 1, proj.shape[1])
    glow3 = glow.reshape(nb, 1, LANES)
    o, s_new = pl.pallas_call(
        kernel,
        grid=(nb // seqs,),
        in_specs=[pl.BlockSpec((seqs, 1, kw), lambda b: (b, 0, q_col // kw)),
                  pl.BlockSpec((seqs, 1, kw), lambda b: (b, 0, k_col // kw)),
                  pl.BlockSpec((seqs, 1, vw), lambda b: (b, 0, v_col // vw)),
                  pl.BlockSpec((seqs, 1, LANES), lambda b: (b, 0, 0)),
                  pl.BlockSpec((LANES, kw), lambda b: (0, 0)),
                  pl.BlockSpec((1, kw), lambda b: (0, 0)),
                  pl.BlockSpec((seqs, heads, dk, dv), lambda b: (b, 0, 0, 0))],
        out_specs=[pl.BlockSpec((seqs, 1, vw), lambda b: (b, 0, 0)),
                   pl.BlockSpec((seqs, heads, dk, dv), lambda b: (b, 0, 0, 0))],
        out_shape=[jax.ShapeDtypeStruct((nb, 1, vw), F32),
                   jax.ShapeDtypeStruct(state.shape, F32)],
        compiler_params=_params("arbitrary"),
        name="gla_step",
    )(proj3, proj3, proj3, glow3, wgu, b_gate, state)
    return o.reshape(nb, vw), s_new


def _out_proj_kernel(osb_ref, og_ref, rg_ref, x_ref, gsb_ref, ggla_ref, w_ref, gpost_ref,
                     gt_ref, gpre_ref, sc_ref, sh_ref, x1_ref, h2_ref, mix_scr, *, dh, dv):
    sbw = osb_ref.shape[1]
    for h in range(sbw // dh):
        sl = slice(h * dh, (h + 1) * dh)
        mix_scr[:, sl] = _rms(osb_ref[:, sl], gsb_ref[...]).astype(BF16)
    for h in range(og_ref.shape[1] // dv):
        sl = slice(h * dv, (h + 1) * dv)
        y = _rms(og_ref[:, sl], ggla_ref[...]) * _silu(rg_ref[:, sl])
        mix_scr[:, sbw + h * dv:sbw + (h + 1) * dv] = y.astype(BF16)
    m = _dot(mix_scr[...], w_ref[...])
    x1 = x_ref[...] + gt_ref[...] * _rms(m, gpost_ref[...])
    x1_ref[...] = x1
    h2 = _rms(x1, gpre_ref[...]) * (1.0 + sc_ref[...]) + sh_ref[...]
    h2_ref[...] = h2.astype(BF16)


def _out_proj(osb, og, proj, rg_col, x, g_sb, g_gla, w, g_post, gt, g_pre, sc, sh, tm,
              rows_per_group):
    m, d = x.shape
    sbw, gw = osb.shape[1], og.shape[1]
    dh, dv = g_sb.shape[1], g_gla.shape[1]
    r = gt.shape[1]
    mod_spec = pl.BlockSpec((None, r, d), lambda i: ((i * tm) // rows_per_group, 0, 0))
    row = lambda width: pl.BlockSpec((1, width), lambda i: (0, 0))
    kernel = functools.partial(_out_proj_kernel, dh=dh, dv=dv)
    return pl.pallas_call(
        kernel,
        grid=(m // tm,),
        in_specs=[pl.BlockSpec((tm, sbw), lambda i: (i, 0)),
                  pl.BlockSpec((tm, gw), lambda i: (i, 0)),
                  pl.BlockSpec((tm, gw), lambda i: (i, rg_col // gw)),
                  pl.BlockSpec((tm, d), lambda i: (i, 0)),
                  row(dh), row(dv),
                  pl.BlockSpec((sbw + gw, d), lambda i: (0, 0)),
                  row(d), mod_spec, row(d), mod_spec, mod_spec],
        out_specs=[pl.BlockSpec((tm, d), lambda i: (i, 0)),
                   pl.BlockSpec((tm, d), lambda i: (i, 0))],
        out_shape=[jax.ShapeDtypeStruct((m, d), F32),
                   jax.ShapeDtypeStruct((m, d), BF16)],
        scratch_shapes=[pltpu.VMEM((tm, sbw + gw), BF16)],
        compiler_params=_params("arbitrary"),
        name="out_proj",
    )(osb, og, proj, x, g_sb, g_gla, w, g_post, gt, g_pre, sc, sh)


def _ffn_kernel(h_ref, wg_ref, wu_ref, wd_ref, x1_ref, gpost_ref, gt_ref, y_ref, acc_scr):
    j = pl.program_id(1)
    last = pl.num_programs(1) - 1

    def part():
        h = h_ref[...]
        t = _silu(_dot(h, wg_ref[...])) * _dot(h, wu_ref[...])
        return _dot(t.astype(BF16), wd_ref[...])

    @pl.when(j == 0)
    def _():
        acc_scr[...] = part()

    @pl.when(jnp.logical_and(j > 0, j < last))
    def _():
        acc_scr[...] += part()

    @pl.when(j == last)
    def _():
        y_ref[...] = x1_ref[...] + gt_ref[...] * _rms(acc_scr[...] + part(), gpost_ref[...])


def _ffn(h2, wg, wu, wd, x1, g_post, gt, tm, rows_per_group, tf=FFN_COL_TILE):
    m, d = x1.shape
    f = wg.shape[1]
    r = gt.shape[1]
    assert f // tf >= 2
    return pl.pallas_call(
        _ffn_kernel,
        grid=(m // tm, f // tf),
        in_specs=[pl.BlockSpec((tm, d), lambda i, j: (i, 0)),
                  pl.BlockSpec((d, tf), lambda i, j: (0, j)),
                  pl.BlockSpec((d, tf), lambda i, j: (0, j)),
                  pl.BlockSpec((tf, d), lambda i, j: (j, 0)),
                  pl.BlockSpec((tm, d), lambda i, j: (i, 0)),
                  pl.BlockSpec((1, d), lambda i, j: (0, 0)),
                  pl.BlockSpec((None, r, d), lambda i, j: ((i * tm) // rows_per_group, 0, 0))],
        out_specs=pl.BlockSpec((tm, d), lambda i, j: (i, 0)),
        out_shape=jax.ShapeDtypeStruct((m, d), F32),
        scratch_shapes=[pltpu.VMEM((tm, d), F32)],
        compiler_params=_params("arbitrary", "arbitrary"),
        name="ffn",
    )(h2, wg, wu, wd, x1, g_post, gt)


def _suffix_sum_matrix():
    j = lax.broadcasted_iota(jnp.int32, (2 * KEY_BLOCK, 2 * KEY_BLOCK), 0) % KEY_BLOCK
    s = lax.broadcasted_iota(jnp.int32, (2 * KEY_BLOCK, 2 * KEY_BLOCK), 1)
    return jnp.where((j > s) | (s >= KEY_BLOCK), 1.0, 0.0).astype(BF16)


def kernel(x_prompt, x_sample, c_prompt, c_sample, cache_k, cache_v, page_table, state_gla, w_ada, b_ada, g_pre_mix, w_in, b_sb, w_gate_up, b_gate, g_sb_out, g_gla_out, w_out, g_post_mix, g_pre_ffn, w_ffn_gate, w_ffn_up, w_ffn_down, g_post_ffn):
    batch, seq, d = x_prompt.shape
    nb = x_sample.shape[0]
    depth, n_pool, page, sb_heads, dh = cache_k.shape
    gla_heads, dk, dv = state_gla.shape[2:]
    rank = w_gate_up.shape[1]
    assert depth == 1 and x_sample.shape[1] == 1
    sbw, kw, vw = sb_heads * dh, gla_heads * dk, gla_heads * dv
    assert 2 * kw == sbw and vw == sbw
    q_col, k_col, v_col, rg_col = 0, kw, sbw, 2 * sbw
    main = 3 * sbw + 2 * kw + 2 * vw

    row = lambda a: a.reshape(1, -1)
    uo = _suffix_sum_matrix()
    ones = jnp.ones((LANES, LANES), BF16)

    w_in0 = w_in[0]
    w_main = w_in0.astype(BF16)
    w_low = jnp.pad(w_in0[:, main:], ((0, 0), (0, LANES - rank))).astype(BF16)
    wgu = jnp.pad(w_gate_up[0], ((0, LANES - rank), (0, 0))).astype(BF16)
    w_o = w_out[0].astype(BF16)
    w_fg, w_fu, w_fd = (w[0].astype(BF16) for w in (w_ffn_gate, w_ffn_up, w_ffn_down))

    n_c = batch + nb
    c_all = jnp.pad(jnp.concatenate([c_prompt, c_sample], axis=0), ((0, -n_c % SUBLANES), (0, 0)))
    mod = _ada(c_all, w_ada[0], row(b_ada[0]))
    mod_p = [a.reshape(batch, 1, d) for a in jnp.split(mod[:batch], 6, axis=-1)]
    mod_s = [a.reshape(1, nb, d) for a in jnp.split(mod[batch:n_c], 6, axis=-1)]

    gains = dict(g_sb=row(g_sb_out[0]), g_gla=row(g_gla_out[0]), g_post=row(g_post_mix[0]),
                 g_pre=row(g_pre_ffn[0]))

    def rest_of_layer(x2, osb, og, proj, mods, tm, rows_per_group, ffn_tm):
        sh1, sc1, gt1, sh2, sc2, gt2 = mods
        x1, h2 = _out_proj(osb, og, proj, rg_col, x2, gains["g_sb"], gains["g_gla"], w_o,
                           gains["g_post"], gt1, gains["g_pre"], sc2, sh2, tm, rows_per_group)
        return _ffn(h2, w_fg, w_fu, w_fd, x1, row(g_post_ffn[0]), gt2, ffn_tm, rows_per_group)

    xp = x_prompt.reshape(batch * seq, d)
    xs = x_sample.reshape(nb, d)
    qkv_p, k_p, v_p, gla_p, glow_p = _in_proj(xp, row(g_pre_mix[0]), mod_p[1], mod_p[0], w_main, w_low,
                                              IN_PROJ_ROW_TILE, seq, sb_heads, dh)
    qkv_s, k_s, v_s, gla_s, glow_s = _in_proj(xs, row(g_pre_mix[0]), mod_s[1], mod_s[0], w_main, w_low,
                                              nb, nb, sb_heads, dh)

    bias_rows = jnp.broadcast_to(b_sb[0][:, None], (sb_heads, KEY_BLOCK))
    osb_p, osb_s = _sb_attention(qkv_p, b_sb[0], uo, qkv_s[0].reshape(nb, sb_heads, dh),
                                 cache_k[0].reshape(n_pool, page * sb_heads, dh),
                                 cache_v[0].reshape(n_pool, page * sb_heads, dh),
                                 page_table, bias_rows, batch, seq, sb_heads, dh,
                                 tq=ATTN_QUERY_TILE)

    og_p, st_p = _gla_prompt(gla_p, glow_p, wgu, row(b_gate[0]), ones, batch, seq, gla_heads,
                             dk, dv, q_col, k_col, v_col)
    y_p = rest_of_layer(xp, osb_p, og_p, gla_p, mod_p, OUT_PROJ_ROW_TILE, seq, FFN_ROW_TILE)
    og_s, st_s = _gla_step(gla_s, glow_s, wgu, row(b_gate[0]), state_gla[0], dk, dv,
                           q_col, k_col, v_col)
    y_s = rest_of_layer(xs, osb_s.reshape(nb, sbw), og_s, gla_s, mod_s, nb, nb, nb)

    kv = lambda a, n: a.reshape(1, n, -1, sb_heads, dh)
    return (y_p.reshape(batch, seq, d), y_s.reshape(nb, 1, d),
            kv(k_p, batch), kv(v_p, batch),
            jnp.swapaxes(st_p, -1, -2)[None],
            kv(k_s, nb), kv(v_s, nb), st_s[None])
```

```python
import functools

import jax
import jax.numpy as jnp
from jax import lax
from jax.experimental import pallas as pl
from jax.experimental.pallas import tpu as pltpu

F32 = jnp.float32
BF16 = jnp.bfloat16

LANES = 128
SUBLANES = 8
VMEM_LIMIT_BYTES = 52 * 1024 * 1024

EPS = 1e-6
LOG2E = 1.4426950408889634
GLA_GATE_NORM = 16.0
GLA_CHUNK = 64
GLA_SUB = 8
KEY_BLOCK = 128

ADA_COL_TILE = 1024
IN_PROJ_ROW_TILE = 1024
ATTN_QUERY_TILE = 512
ATTN_KEY_GROUP = 8
V_RING_SLOTS = 3
GLA_CHUNKS_PER_STEP = 32
GLA_STEP_SEQS = 4
OUT_PROJ_ROW_TILE = 256
FFN_ROW_TILE = 512
FFN_COL_TILE = 512


def _params(*sem):
    return pltpu.CompilerParams(dimension_semantics=sem, vmem_limit_bytes=VMEM_LIMIT_BYTES)


def _log_sigmoid(z):
    return jnp.minimum(z, 0.0) - jnp.log(1.0 + jnp.exp(-jnp.abs(z)))


def _log2_sigmoid(z2):
    return jnp.minimum(z2, 0.0) - jnp.log(1.0 + jnp.exp2(-jnp.abs(z2))) * LOG2E


def _silu(x):
    return x * (1.0 / (1.0 + jnp.exp(-x)))


def _rms(x, g):
    ms = jnp.mean(x * x, axis=-1, keepdims=True)
    return x * lax.rsqrt(ms + EPS) * g


def _split_bf16(x, pieces):
    out = []
    for _ in range(pieces - 1):
        p = x.astype(BF16)
        out.append(p)
        x = x - p.astype(F32)
    out.append(x.astype(BF16))
    return out


def _dot(a, b):
    return jnp.dot(a, b, preferred_element_type=F32)


def _dot_nt(a, b):
    return lax.dot_general(a, b, (((1,), (1,)), ((), ())), preferred_element_type=F32)


def _dot_tn(a, b):
    return lax.dot_general(a, b, (((0,), (0,)), ((), ())), preferred_element_type=F32)


def _ada_kernel(c_ref, w_ref, b_ref, o_ref):
    a = _silu(c_ref[...]).astype(BF16)
    o_ref[...] = _dot(a, w_ref[...].astype(BF16)) + b_ref[...]


def _ada(c, w, b, tn=ADA_COL_TILE):
    m, d = c.shape
    n = w.shape[1]
    return pl.pallas_call(
        _ada_kernel,
        grid=(n // tn,),
        in_specs=[pl.BlockSpec((m, d), lambda j: (0, 0)),
                  pl.BlockSpec((d, tn), lambda j: (0, j)),
                  pl.BlockSpec((1, tn), lambda j: (0, j))],
        out_specs=pl.BlockSpec((m, tn), lambda j: (0, j)),
        out_shape=jax.ShapeDtypeStruct((m, n), F32),
        compiler_params=_params("arbitrary"),
        name="ada_mod",
    )(c, w, b)


def _prenorm(x_ref, g_ref, sc_ref, sh_ref, h_scr):
    h = _rms(x_ref[...], g_ref[...]) * (1.0 + sc_ref[...]) + sh_ref[...]
    h_scr[...] = h.astype(BF16)


def _in_proj_attn_kernel(x_ref, g_ref, sc_ref, sh_ref, w_ref, qkv_ref, krow_ref, vrow_ref, h_scr,
                         *, heads, dh, q_scale):
    j = pl.program_id(1)

    def product():
        return _dot(h_scr[...], w_ref[...])

    @pl.when(j == 0)
    def _():
        _prenorm(x_ref, g_ref, sc_ref, sh_ref, h_scr)
        qkv_ref[...] = (product() * q_scale).astype(BF16)

    for tile, rows_ref in ((1, krow_ref), (2, vrow_ref)):
        @pl.when(j == tile)
        def _(rows_ref=rows_ref):
            res = product()
            qkv_ref[...] = res.astype(BF16)
            for h in range(heads):
                rows_ref[pl.ds(h, res.shape[0], stride=heads), :] = res[:, h * dh:(h + 1) * dh]


def _in_proj_gla_kernel(x_ref, g_ref, sc_ref, sh_ref, w_ref, wl_ref, gla_ref, ol_ref, h_scr):
    @pl.when(pl.program_id(1) == 0)
    def _():
        _prenorm(x_ref, g_ref, sc_ref, sh_ref, h_scr)
        ol_ref[...] = _dot(h_scr[...], wl_ref[...])

    gla_ref[...] = _dot(h_scr[...], w_ref[...])


def _in_proj(x, g, sc, sh, w, wl, tm, rows_per_group, heads, dh):
    m, d = x.shape
    tn = heads * dh
    r = sc.shape[1]
    mod_spec = pl.BlockSpec((None, r, d), lambda i, j: ((i * tm) // rows_per_group, 0, 0))
    common = [pl.BlockSpec((tm, d), lambda i, j: (i, 0)),
              pl.BlockSpec((1, d), lambda i, j: (0, 0)),
              mod_spec, mod_spec]
    attn_kernel = functools.partial(_in_proj_attn_kernel, heads=heads, dh=dh,
                                    q_scale=dh ** -0.5 * LOG2E)
    qkv, krow, vrow = pl.pallas_call(
        attn_kernel,
        grid=(m // tm, 3),
        in_specs=common + [pl.BlockSpec((d, tn), lambda i, j: (0, j))],
        out_specs=[pl.BlockSpec((None, tm, tn), lambda i, j: (j, i, 0)),
                   pl.BlockSpec((tm * heads, dh), lambda i, j: (i, 0)),
                   pl.BlockSpec((tm * heads, dh), lambda i, j: (i, 0))],
        out_shape=[jax.ShapeDtypeStruct((3, m, tn), BF16),
                   jax.ShapeDtypeStruct((m * heads, dh), F32),
                   jax.ShapeDtypeStruct((m * heads, dh), F32)],
        scratch_shapes=[pltpu.VMEM((tm, d), BF16)],
        compiler_params=_params("arbitrary", "arbitrary"),
        name="in_proj_attn",
    )(x, g, sc, sh, w)
    gla, glow = pl.pallas_call(
        _in_proj_gla_kernel,
        grid=(m // tm, 3),
        in_specs=common + [pl.BlockSpec((d, tn), lambda i, j: (0, 3 + j)),
                           pl.BlockSpec((d, LANES), lambda i, j: (0, 0))],
        out_specs=[pl.BlockSpec((tm, tn), lambda i, j: (i, j)),
                   pl.BlockSpec((tm, LANES), lambda i, j: (i, 0))],
        out_shape=[jax.ShapeDtypeStruct((m, 3 * tn), F32),
                   jax.ShapeDtypeStruct((m, LANES), F32)],
        scratch_shapes=[pltpu.VMEM((tm, d), BF16)],
        compiler_params=_params("arbitrary", "arbitrary"),
        name="in_proj_gla",
    )(x, g, sc, sh, w, wl)
    return qkv, krow, vrow, gla, glow


def _sb_group(z, carry, uo, mask):
    n = z.shape[1] // KEY_BLOCK
    lb = _log2_sigmoid(z)
    lk = lb - z
    if mask is not None:
        lk = jnp.where(mask, lk, 0.0)
    hi, lo = _split_bf16(lk, 2)
    tails = [None] * n
    for s in reversed(range(n)):
        sl = slice(s * KEY_BLOCK, (s + 1) * KEY_BLOCK)
        ts = _dot(jnp.concatenate([hi[:, sl], lo[:, sl]], axis=1), uo)
        tails[s] = ts[:, :KEY_BLOCK] + carry
        carry = carry + ts[:, KEY_BLOCK:]
    a = jnp.exp2(lb + jnp.concatenate(tails, axis=1))
    if mask is not None:
        a = jnp.where(mask, a, 0.0)
    return a.astype(BF16), carry


def _sb_decode_step(q_ref, bias_ref, uo_ref, gather_ref, spread_ref, k_refs, v_refs, carry_scr,
                    acc_scr, heads):
    pages = len(k_refs)
    q = q_ref[...]
    rows = k_refs[0].shape[0]
    own = (lax.broadcasted_iota(jnp.int32, (pages * heads, rows), 1) % heads
           == lax.broadcasted_iota(jnp.int32, (pages * heads, rows), 0) % heads)
    z_all = jnp.concatenate([_dot_nt(q, r[...].astype(BF16)) for r in k_refs], axis=0)
    pieces = _split_bf16(jnp.where(own, z_all, 0.0), 2)
    zr = _dot(jnp.concatenate(pieces, axis=1), gather_ref[...])
    zr = zr + jnp.concatenate([bias_ref[...] * LOG2E] * pages, axis=0)
    lb = _log2_sigmoid(zr)
    hi, lo = _split_bf16(lb - zr, 2)
    ts = _dot(jnp.concatenate([hi, lo], axis=1), uo_ref[...])
    carry = carry_scr[...]
    tails = []
    for p in range(pages):
        tsp = ts[p * heads:(p + 1) * heads]
        tails.append(tsp[:, :KEY_BLOCK] + carry)
        carry = carry + tsp[:, KEY_BLOCK:]
    carry_scr[...] = carry
    a = jnp.exp2(lb + jnp.concatenate(tails, axis=0)).astype(BF16)
    a_rows = jnp.where(own, _dot(a, spread_ref[...]), 0.0)
    acc = acc_scr[...]
    for p in range(pages):
        acc = acc + _dot(a_rows[p * heads:(p + 1) * heads].astype(BF16), v_refs[p][...].astype(BF16))
    acc_scr[...] = acc


def _sb_attn_kernel(pt_ref, q_ref, k_ref, v_ref, bias_ref, uo_ref, qs_ref, bias_rows_ref,
                    gather_ref, spread_ref, *refs, tq, group, pages, heads):
    k_refs, cache_v_ref = refs[:pages], refs[pages]
    (o_ref, os_ref, carry_scr, acc_scr, dcarry_scr, dacc_scr, v_ring,
     v_sems) = refs[pages + 1:]
    qi = pl.program_id(2)
    h = pl.program_id(1)
    nq = pl.num_programs(2)
    n_pages = pt_ref.shape[1]

    step = (pl.program_id(0) * pl.num_programs(1) + h) * nq + qi
    n_steps = pl.num_programs(0) * pl.num_programs(1) * nq

    def v_copies(s):
        slot = lax.rem(s, V_RING_SLOTS)
        seq, grp = lax.div(s, nq), lax.rem(s, nq)
        return [pltpu.make_async_copy(
            cache_v_ref.at[pt_ref[seq, n_pages - 1 - (grp * pages + j)]],
            v_ring.at[slot, j], v_sems.at[slot]) for j in range(pages)]

    @pl.when(step == 0)
    def _():
        for s in range(V_RING_SLOTS - 1):
            for c in v_copies(jnp.int32(s)):
                c.start()

    @pl.when(step + V_RING_SLOTS - 1 < n_steps)
    def _():
        for c in v_copies(step + V_RING_SLOTS - 1):
            c.start()

    for c in v_copies(step):
        c.wait()
    v_slot = lax.rem(step, V_RING_SLOTS)
    v_refs = [v_ring.at[v_slot, j] for j in range(pages)]
    q = q_ref[...]
    bias = bias_ref[h] * LOG2E
    uo = uo_ref[...]
    n_diag = tq // KEY_BLOCK

    @pl.when(qi == 0)
    def _():
        dcarry_scr[...] = jnp.zeros_like(dcarry_scr)
        dacc_scr[...] = jnp.zeros_like(dacc_scr)

    def visit(first_block, n, newest):
        rows = n * KEY_BLOCK
        start = pl.multiple_of(first_block * KEY_BLOCK, KEY_BLOCK)
        kb = k_ref[pl.ds(start, rows), :]
        vb = v_ref[pl.ds(start, rows), :]
        mask = None
        if newest:
            mask = (lax.broadcasted_iota(jnp.int32, (tq, rows), 1) + (start - qi * tq)
                    < lax.broadcasted_iota(jnp.int32, (tq, rows), 0))
            _sb_decode_step(qs_ref, bias_rows_ref, uo_ref, gather_ref, spread_ref, k_refs, v_refs,
                            dcarry_scr, dacc_scr, heads)
        a, carry = _sb_group(_dot_nt(q, kb) + bias, carry_scr[...], uo, mask)
        carry_scr[...] = carry
        acc_scr[...] += _dot(a, vb)

    carry_scr[...] = jnp.zeros_like(carry_scr)
    acc_scr[...] = jnp.zeros_like(acc_scr)
    n_tot = (qi + 1) * n_diag
    n_groups = n_tot // group
    leftover = lax.rem(n_tot, group)

    @pl.when(n_groups > 0)
    def _():
        visit(n_tot - group, group, True)

    def body(g, _):
        visit(n_tot - (g + 1) * group, group, False)
        return 0

    lax.fori_loop(1, n_groups, body, 0)
    size = group // 2
    while size >= n_diag:
        take = lax.rem(lax.div(leftover, size), 2) == 1
        is_first = jnp.logical_and(n_groups == 0, lax.div(leftover, 2 * size) == 0)
        for newest in (True, False):
            pl.when(jnp.logical_and(take, is_first == newest))(
                functools.partial(visit, lax.rem(leftover, size), size, newest))
        size //= 2
    o_ref[...] = acc_scr[...]

    @pl.when(qi == pl.num_programs(2) - 1)
    def _():
        os_ref[...] = dacc_scr[...]


def _sb_attention(qkv, b_sb, uo, q_s, cache_k, cache_v, page_table, bias_rows, batch, seq, heads,
                  dh, tq, group=ATTN_KEY_GROUP):
    m = qkv.shape[1]
    nq = seq // tq
    nb, n_pages = page_table.shape
    rows = cache_k.shape[1]
    assert nb == batch * heads and n_pages % nq == 0 and rows == KEY_BLOCK * heads
    pages = n_pages // nq
    kernel = functools.partial(_sb_attn_kernel, tq=tq, group=group, pages=pages, heads=heads)
    key_of_row = lax.broadcasted_iota(jnp.int32, (rows, KEY_BLOCK), 0) // heads
    gather1 = jnp.where(key_of_row == lax.broadcasted_iota(jnp.int32, (rows, KEY_BLOCK), 1), 1.0, 0.0)
    gather = jnp.concatenate([gather1] * 2, axis=0).astype(BF16)
    spread = gather1.T.astype(BF16)

    def page_spec(j):
        return pl.BlockSpec(
            (None, rows, dh),
            lambda b, h, i, pt, j=j: (pt[b * heads + h, n_pages - 1 - (i * pages + j)], 0, 0))

    const = lambda a: pl.BlockSpec(a.shape, lambda b, h, i, pt: (0,) * a.ndim)
    grid_spec = pltpu.PrefetchScalarGridSpec(
        num_scalar_prefetch=1,
        grid=(batch, heads, nq),
        in_specs=[pl.BlockSpec((None, tq, dh), lambda b, h, i, pt: (0, b * nq + i, h)),
                  pl.BlockSpec((None, seq, dh), lambda b, h, i, pt: (1, b, h)),
                  pl.BlockSpec((None, seq, dh), lambda b, h, i, pt: (2, b, h)),
                  pl.BlockSpec(memory_space=pltpu.SMEM),
                  const(uo),
                  pl.BlockSpec((None, heads, dh), lambda b, h, i, pt: (b * heads + h, 0, 0)),
                  const(bias_rows), const(gather), const(spread)]
                 + [page_spec(j) for j in range(pages)]
                 + [pl.BlockSpec(memory_space=pl.ANY)],
        out_specs=[pl.BlockSpec((tq, dh), lambda b, h, i, pt: (b * nq + i, h)),
                   pl.BlockSpec((None, heads, dh), lambda b, h, i, pt: (b * heads + h, 0, 0))],
        scratch_shapes=[pltpu.VMEM((tq, KEY_BLOCK), F32), pltpu.VMEM((tq, dh), F32),
                        pltpu.VMEM((heads, KEY_BLOCK), F32), pltpu.VMEM((heads, dh), F32),
                        pltpu.VMEM((V_RING_SLOTS, pages, rows, dh), F32),
                        pltpu.SemaphoreType.DMA((V_RING_SLOTS,))],
    )
    return pl.pallas_call(
        kernel,
        grid_spec=grid_spec,
        out_shape=[jax.ShapeDtypeStruct((m, heads * dh), F32),
                   jax.ShapeDtypeStruct((nb, heads, dh), F32)],
        compiler_params=_params("arbitrary", "arbitrary", "arbitrary"),
        name="sb_attention",
    )(page_table, qkv, qkv, qkv, b_sb, uo, q_s, bias_rows, gather, spread,
      *([cache_k] * pages), cache_v)


def _gla_prefix_matrix():
    t = lax.broadcasted_iota(jnp.int32, (GLA_CHUNK, GLA_CHUNK), 0)
    s = lax.broadcasted_iota(jnp.int32, (GLA_CHUNK, GLA_CHUNK), 1)
    return jnp.where(s <= t, 1.0, 0.0).astype(BF16)


def _gla_intra(qs, ks, bs, ones_bf16):
    n = len(qs)
    c = GLA_CHUNK
    t_i = lax.broadcasted_iota(jnp.int32, (c, c), 0)
    s_i = lax.broadcasted_iota(jnp.int32, (c, c), 1)
    atts = [jnp.zeros((c, c), F32)] * n
    h = c // 2
    while h >= GLA_SUB:
        th, sh = t_i // h, s_i // h
        pair = ((th - sh - 1) | ((th & 1) ^ 1)) == 0
        for ci in range(n):
            q, k, b = qs[ci], ks[ci], bs[ci]
            ref = jnp.concatenate(
                [jnp.broadcast_to(b[j + h - 1:j + h], (2 * h, b.shape[1])) for j in range(0, c, 2 * h)],
                axis=0)
            ql = (q * jnp.exp(jnp.minimum(b - ref, 0.0))).astype(BF16)
            kl = (k * jnp.exp(jnp.minimum(ref - b, 0.0))).astype(BF16)
            atts[ci] = atts[ci] + jnp.where(pair, _dot_nt(ql, kl), 0.0)
        h //= 2
    prods = []
    for ci in range(n):
        q, k, b = qs[ci], ks[ci], bs[ci]
        for lo_ in range(0, c, GLA_SUB):
            qi, ki, bi = (a[lo_:lo_ + GLA_SUB] for a in (q, k, b))
            prods += [qi * ki[s:s + 1] * jnp.exp(jnp.minimum(bi - bi[s:s + 1], 0.0))
                      for s in range(GLA_SUB)]
    rsum = _dot(jnp.concatenate(prods, axis=0).astype(BF16), ones_bf16)
    lane = lax.broadcasted_iota(jnp.int32, (GLA_SUB, LANES), 1)
    trow = lax.broadcasted_iota(jnp.int32, (GLA_SUB, LANES), 0)
    for ci in range(n):
        strips = []
        for i, lo_ in enumerate(range(0, c, GLA_SUB)):
            base = (ci * (c // GLA_SUB) + i) * GLA_SUB * GLA_SUB
            strip = jnp.zeros((GLA_SUB, LANES), F32)
            for s in range(GLA_SUB):
                strip = jnp.where(lane == lo_ + s,
                                  rsum[base + s * GLA_SUB:base + (s + 1) * GLA_SUB], strip)
            strips.append(jnp.where(lane - lo_ <= trow, strip, 0.0)[:, :c])
        atts[ci] = atts[ci] + jnp.concatenate(strips, axis=0)
    return atts


def _gla_prompt_kernel(q_ref, k_ref, v_ref, gl_ref, wg_ref, bg_ref, ones_ref, ll_ref, o_ref,
                       st_ref, st_scr, *, n_inner, scale):
    @pl.when(pl.program_id(2) == 0)
    def _():
        st_scr[...] = jnp.zeros_like(st_scr)

    c = GLA_CHUNK
    ll = ll_ref[...]
    g = _log_sigmoid(_dot(gl_ref[...].astype(BF16), wg_ref[...]) + bg_ref[...]) * (1.0 / GLA_GATE_NORM)
    chunks = [slice(ci * c, (ci + 1) * c) for ci in range(n_inner)]
    bs = [sum(_dot(ll, piece) for piece in _split_bf16(g[sl], 3)) for sl in chunks]
    qs = [q_ref[sl, :] * scale for sl in chunks]
    ks = [k_ref[sl, :] for sl in chunks]
    atts = _gla_intra(qs, ks, bs, ones_ref[...])
    st = st_scr[...]
    for ci, sl in enumerate(chunks):
        q, k, b = qs[ci], ks[ci], bs[ci]
        v16 = v_ref[sl, :].astype(BF16)
        o_ref[sl, :] = (_dot_nt((q * jnp.exp(b)).astype(BF16), st.astype(BF16))
                        + _dot(atts[ci].astype(BF16), v16))
        b_last = b[c - 1:c]
        kdec = (k * jnp.exp(b_last - b)).astype(BF16)
        st = st * jnp.exp(b_last) + _dot_tn(v16, kdec)
    st_scr[...] = st
    st_ref[...] = st


def _gla_prompt(proj, glow, wgu, b_gate, ones, batch, seq, heads, dk, dv, q_col, k_col, v_col,
                n_inner=GLA_CHUNKS_PER_STEP):
    m = proj.shape[0]
    n_inner = min(n_inner, seq // GLA_CHUNK)
    cb = n_inner * GLA_CHUNK
    nc = seq // cb
    kernel = functools.partial(_gla_prompt_kernel, n_inner=n_inner, scale=dk ** -0.5)
    return pl.pallas_call(
        kernel,
        grid=(batch, heads, nc),
        in_specs=[pl.BlockSpec((cb, dk), lambda b, h, c: (b * nc + c, q_col // dk + h)),
                  pl.BlockSpec((cb, dk), lambda b, h, c: (b * nc + c, k_col // dk + h)),
                  pl.BlockSpec((cb, dv), lambda b, h, c: (b * nc + c, v_col // dv + h)),
                  pl.BlockSpec((cb, LANES), lambda b, h, c: (b * nc + c, 0)),
                  pl.BlockSpec((LANES, dk), lambda b, h, c: (0, h)),
                  pl.BlockSpec((1, dk), lambda b, h, c: (0, h)),
                  pl.BlockSpec((LANES, LANES), lambda b, h, c: (0, 0)),
                  pl.BlockSpec((GLA_CHUNK, GLA_CHUNK), lambda b, h, c: (0, 0))],
        out_specs=[pl.BlockSpec((cb, dv), lambda b, h, c: (b * nc + c, h)),
                   pl.BlockSpec((None, None, dv, dk), lambda b, h, c: (b, h, 0, 0))],
        out_shape=[jax.ShapeDtypeStruct((m, heads * dv), F32),
                   jax.ShapeDtypeStruct((batch, heads, dv, dk), F32)],
        scratch_shapes=[pltpu.VMEM((dv, dk), F32)],
        compiler_params=_params("arbitrary", "arbitrary", "arbitrary"),
        name="gla_prompt",
    )(proj, proj, proj, glow, wgu, b_gate, ones, _gla_prefix_matrix())


def _gla_step_kernel(q_ref, k_ref, v_ref, gl_ref, wg_ref, bg_ref, s_ref, o_ref, so_ref, *,
                     heads, dk, dv, scale):
    eye = (lax.broadcasted_iota(jnp.int32, (dk, dk), 0)
           == lax.broadcasted_iota(jnp.int32, (dk, dk), 1))

    def column(x_row):
        return jnp.sum(jnp.where(eye, x_row, 0.0), axis=1, keepdims=True)

    for i in range(q_ref.shape[0]):
        g = _log_sigmoid(_dot(gl_ref[i].astype(BF16), wg_ref[...]) + bg_ref[...]) * (1.0 / GLA_GATE_NORM)
        q = q_ref[i] * scale
        k = k_ref[i]
        v = v_ref[i]
        for h in range(heads):
            ksl = slice(h * dk, (h + 1) * dk)
            vsl = slice(h * dv, (h + 1) * dv)
            s_new = column(jnp.exp(g[:, ksl])) * s_ref[i, h] + column(k[:, ksl]) * v[:, vsl]
            so_ref[i, h] = s_new
            o_ref[i, :, vsl] = jnp.sum(column(q[:, ksl]) * s_new, axis=0, keepdims=True)


def _gla_step(proj, glow, wgu, b_gate, state, dk, dv, q_col, k_col, v_col, seqs=GLA_STEP_SEQS):
    nb, heads = state.shape[0], state.shape[1]
    kw, vw = heads * dk, heads * dv
    kernel = functools.partial(_gla_step_kernel, heads=heads, dk=dk, dv=dv, scale=dk ** -0.5)
    proj3 = proj.reshape(nb, 1, proj.shape[1])
    glow3 = glow.reshape(nb, 1, LANES)
    o, s_new = pl.pallas_call(
        kernel,
        grid=(nb // seqs,),
        in_specs=[pl.BlockSpec((seqs, 1, kw), lambda b: (b, 0, q_col // kw)),
                  pl.BlockSpec((seqs, 1, kw), lambda b: (b, 0, k_col // kw)),
                  pl.BlockSpec((seqs, 1, vw), lambda b: (b, 0, v_col // vw)),
                  pl.BlockSpec((seqs, 1, LANES), lambda b: (b, 0, 0)),
                  pl.BlockSpec((LANES, kw), lambda b: (0, 0)),
                  pl.BlockSpec((1, kw), lambda b: (0, 0)),
                  pl.BlockSpec((seqs, heads, dk, dv), lambda b: (b, 0, 0, 0))],
        out_specs=[pl.BlockSpec((seqs, 1, vw), lambda b: (b, 0, 0)),
                   pl.BlockSpec((seqs, heads, dk, dv), lambda b: (b, 0, 0, 0))],
        out_shape=[jax.ShapeDtypeStruct((nb, 1, vw), F32),
                   jax.ShapeDtypeStruct(state.shape, F32)],
        compiler_params=_params("arbitrary"),
        name="gla_step",
    )(proj3, proj3, proj3, glow3, wgu, b_gate, state)
    return o.reshape(nb, vw), s_new


def _out_proj_kernel(osb_ref, og_ref, rg_ref, x_ref, gsb_ref, ggla_ref, w_ref, gpost_ref,
                     gt_ref, gpre_ref, sc_ref, sh_ref, x1_ref, h2_ref, mix_scr, *, dh, dv):
    sbw = osb_ref.shape[1]
    for h in range(sbw // dh):
        sl = slice(h * dh, (h + 1) * dh)
        mix_scr[:, sl] = _rms(osb_ref[:, sl], gsb_ref[...]).astype(BF16)
    for h in range(og_ref.shape[1] // dv):
        sl = slice(h * dv, (h + 1) * dv)
        y = _rms(og_ref[:, sl], ggla_ref[...]) * _silu(rg_ref[:, sl])
        mix_scr[:, sbw + h * dv:sbw + (h + 1) * dv] = y.astype(BF16)
    m = _dot(mix_scr[...], w_ref[...])
    x1 = x_ref[...] + gt_ref[...] * _rms(m, gpost_ref[...])
    x1_ref[...] = x1
    h2 = _rms(x1, gpre_ref[...]) * (1.0 + sc_ref[...]) + sh_ref[...]
    h2_ref[...] = h2.astype(BF16)


def _out_proj(osb, og, proj, rg_col, x, g_sb, g_gla, w, g_post, gt, g_pre, sc, sh, tm,
              rows_per_group):
    m, d = x.shape
    sbw, gw = osb.shape[1], og.shape[1]
    dh, dv = g_sb.shape[1], g_gla.shape[1]
    r = gt.shape[1]
    mod_spec = pl.BlockSpec((None, r, d), lambda i: ((i * tm) // rows_per_group, 0, 0))
    row = lambda width: pl.BlockSpec((1, width), lambda i: (0, 0))
    kernel = functools.partial(_out_proj_kernel, dh=dh, dv=dv)
    return pl.pallas_call(
        kernel,
        grid=(m // tm,),
        in_specs=[pl.BlockSpec((tm, sbw), lambda i: (i, 0)),
                  pl.BlockSpec((tm, gw), lambda i: (i, 0)),
                  pl.BlockSpec((tm, gw), lambda i: (i, rg_col // gw)),
                  pl.BlockSpec((tm, d), lambda i: (i, 0)),
                  row(dh), row(dv),
                  pl.BlockSpec((sbw + gw, d), lambda i: (0, 0)),
                  row(d), mod_spec, row(d), mod_spec, mod_spec],
        out_specs=[pl.BlockSpec((tm, d), lambda i: (i, 0)),
                   pl.BlockSpec((tm, d), lambda i: (i, 0))],
        out_shape=[jax.ShapeDtypeStruct((m, d), F32),
                   jax.ShapeDtypeStruct((m, d), BF16)],
        scratch_shapes=[pltpu.VMEM((tm, sbw + gw), BF16)],
        compiler_params=_params("arbitrary"),
        name="out_proj",
    )(osb, og, proj, x, g_sb, g_gla, w, g_post, gt, g_pre, sc, sh)


def _ffn_kernel(h_ref, wg_ref, wu_ref, wd_ref, x1_ref, gpost_ref, gt_ref, y_ref, acc_scr):
    j = pl.program_id(1)
    last = pl.num_programs(1) - 1

    def part():
        h = h_ref[...]
        t = _silu(_dot(h, wg_ref[...])) * _dot(h, wu_ref[...])
        return _dot(t.astype(BF16), wd_ref[...])

    @pl.when(j == 0)
    def _():
        acc_scr[...] = part()

    @pl.when(jnp.logical_and(j > 0, j < last))
    def _():
        acc_scr[...] += part()

    @pl.when(j == last)
    def _():
        y_ref[...] = x1_ref[...] + gt_ref[...] * _rms(acc_scr[...] + part(), gpost_ref[...])


def _ffn(h2, wg, wu, wd, x1, g_post, gt, tm, rows_per_group, tf=FFN_COL_TILE):
    m, d = x1.shape
    f = wg.shape[1]
    r = gt.shape[1]
    assert f // tf >= 2
    return pl.pallas_call(
        _ffn_kernel,
        grid=(m // tm, f // tf),
        in_specs=[pl.BlockSpec((tm, d), lambda i, j: (i, 0)),
                  pl.BlockSpec((d, tf), lambda i, j: (0, j)),
                  pl.BlockSpec((d, tf), lambda i, j: (0, j)),
                  pl.BlockSpec((tf, d), lambda i, j: (j, 0)),
                  pl.BlockSpec((tm, d), lambda i, j: (i, 0)),
                  pl.BlockSpec((1, d), lambda i, j: (0, 0)),
                  pl.BlockSpec((None, r, d), lambda i, j: ((i * tm) // rows_per_group, 0, 0))],
        out_specs=pl.BlockSpec((tm, d), lambda i, j: (i, 0)),
        out_shape=jax.ShapeDtypeStruct((m, d), F32),
        scratch_shapes=[pltpu.VMEM((tm, d), F32)],
        compiler_params=_params("arbitrary", "arbitrary"),
        name="ffn",
    )(h2, wg, wu, wd, x1, g_post, gt)


def _suffix_sum_matrix():
    j = lax.broadcasted_iota(jnp.int32, (2 * KEY_BLOCK, 2 * KEY_BLOCK), 0) % KEY_BLOCK
    s = lax.broadcasted_iota(jnp.int32, (2 * KEY_BLOCK, 2 * KEY_BLOCK), 1)
    return jnp.where((j > s) | (s >= KEY_BLOCK), 1.0, 0.0).astype(BF16)


def kernel(x_prompt, x_sample, c_prompt, c_sample, cache_k, cache_v, page_table, state_gla, w_ada, b_ada, g_pre_mix, w_in, b_sb, w_gate_up, b_gate, g_sb_out, g_gla_out, w_out, g_post_mix, g_pre_ffn, w_ffn_gate, w_ffn_up, w_ffn_down, g_post_ffn):
    batch, seq, d = x_prompt.shape
    nb = x_sample.shape[0]
    depth, n_pool, page, sb_heads, dh = cache_k.shape
    gla_heads, dk, dv = state_gla.shape[2:]
    rank = w_gate_up.shape[1]
    assert depth == 1 and x_sample.shape[1] == 1
    sbw, kw, vw = sb_heads * dh, gla_heads * dk, gla_heads * dv
    assert 2 * kw == sbw and vw == sbw
    q_col, k_col, v_col, rg_col = 0, kw, sbw, 2 * sbw
    main = 3 * sbw + 2 * kw + 2 * vw

    row = lambda a: a.reshape(1, -1)
    uo = _suffix_sum_matrix()
    ones = jnp.ones((LANES, LANES), BF16)

    w_in0 = w_in[0]
    w_main = w_in0.astype(BF16)
    w_low = jnp.pad(w_in0[:, main:], ((0, 0), (0, LANES - rank))).astype(BF16)
    wgu = jnp.pad(w_gate_up[0], ((0, LANES - rank), (0, 0))).astype(BF16)
    w_o = w_out[0].astype(BF16)
    w_fg, w_fu, w_fd = (w[0].astype(BF16) for w in (w_ffn_gate, w_ffn_up, w_ffn_down))

    n_c = batch + nb
    c_all = jnp.pad(jnp.concatenate([c_prompt, c_sample], axis=0), ((0, -n_c % SUBLANES), (0, 0)))
    mod = _ada(c_all, w_ada[0], row(b_ada[0]))
    mod_p = [a.reshape(batch, 1, d) for a in jnp.split(mod[:batch], 6, axis=-1)]
    mod_s = [a.reshape(1, nb, d) for a in jnp.split(mod[batch:n_c], 6, axis=-1)]

    gains = dict(g_sb=row(g_sb_out[0]), g_gla=row(g_gla_out[0]), g_post=row(g_post_mix[0]),
                 g_pre=row(g_pre_ffn[0]))

    def rest_of_layer(x2, osb, og, proj, mods, tm, rows_per_group, ffn_tm):
        sh1, sc1, gt1, sh2, sc2, gt2 = mods
        x1, h2 = _out_proj(osb, og, proj, rg_col, x2, gains["g_sb"], gains["g_gla"], w_o,
                           gains["g_post"], gt1, gains["g_pre"], sc2, sh2, tm, rows_per_group)
        return _ffn(h2, w_fg, w_fu, w_fd, x1, row(g_post_ffn[0]), gt2, ffn_tm, rows_per_group)

    xp = x_prompt.reshape(batch * seq, d)
    xs = x_sample.reshape(nb, d)
    qkv_p, k_p, v_p, gla_p, glow_p = _in_proj(xp, row(g_pre_mix[0]), mod_p[1], mod_p[0], w_main, w_low,
                                              IN_PROJ_ROW_TILE, seq, sb_heads, dh)
    qkv_s, k_s, v_s, gla_s, glow_s = _in_proj(xs, row(g_pre_mix[0]), mod_s[1], mod_s[0], w_main, w_low,
                                              nb, nb, sb_heads, dh)

    bias_rows = jnp.broadcast_to(b_sb[0][:, None], (sb_heads, KEY_BLOCK))
    osb_p, osb_s = _sb_attention(qkv_p, b_sb[0], uo, qkv_s[0].reshape(nb, sb_heads, dh),
                                 cache_k[0].reshape(n_pool, page * sb_heads, dh),
                                 cache_v[0].reshape(n_pool, page * sb_heads, dh),
                                 page_table, bias_rows, batch, seq, sb_heads, dh,
                                 tq=ATTN_QUERY_TILE)

    og_p, st_p = _gla_prompt(gla_p, glow_p, wgu, row(b_gate[0]), ones, batch, seq, gla_heads,
                             dk, dv, q_col, k_col, v_col)
    y_p = rest_of_layer(xp, osb_p, og_p, gla_p, mod_p, OUT_PROJ_ROW_TILE, seq, FFN_ROW_TILE)
    og_s, st_s = _gla_step(gla_s, glow_s, wgu, row(b_gate[0]), state_gla[0], dk, dv,
                           q_col, k_col, v_col)
    y_s = rest_of_layer(xs, osb_s.reshape(nb, sbw), og_s, gla_s, mod_s, nb, nb, nb)

    kv = lambda a, n: a.reshape(1, n, -1, sb_heads, dh)
    return (y_p.reshape(batch, seq, d), y_s.reshape(nb, 1, d),
            kv(k_p, batch), kv(v_p, batch),
            jnp.swapaxes(st_p, -1, -2)[None],
            kv(k_s, nb), kv(v_s, nb), st_s[None])
```

```python
import functools

import jax
import jax.numpy as jnp
from jax import lax
from jax.experimental import pallas as pl
from jax.experimental.pallas import tpu as pltpu

F32 = jnp.float32
BF16 = jnp.bfloat16

LANES = 128
SUBLANES = 8
VMEM_LIMIT_BYTES = 52 * 1024 * 1024

EPS = 1e-6
LOG2E = 1.4426950408889634
GLA_GATE_NORM = 16.0
GLA_CHUNK = 64
GLA_SUB = 8
KEY_BLOCK = 128

ADA_COL_TILE = 1024
IN_PROJ_ROW_TILE = 1024
ATTN_QUERY_TILE = 512
ATTN_KEY_GROUP = 8
V_RING_SLOTS = 3
GLA_CHUNKS_PER_STEP = 32
GLA_STEP_SEQS = 4
OUT_PROJ_ROW_TILE = 256
FFN_ROW_TILE = 512
FFN_COL_TILE = 512


def _params(*sem):
    return pltpu.CompilerParams(dimension_semantics=sem, vmem_limit_bytes=VMEM_LIMIT_BYTES)


def _log_sigmoid(z):
    return jnp.minimum(z, 0.0) - jnp.log(1.0 + jnp.exp(-jnp.abs(z)))


def _log2_sigmoid(z2):
    return jnp.minimum(z2, 0.0) - jnp.log(1.0 + jnp.exp2(-jnp.abs(z2))) * LOG2E


def _silu(x):
    return x * (1.0 / (1.0 + jnp.exp(-x)))


def _rms(x, g):
    ms = jnp.mean(x * x, axis=-1, keepdims=True)
    return x * lax.rsqrt(ms + EPS) * g


def _split_bf16(x, pieces):
    out = []
    for _ in range(pieces - 1):
        p = x.astype(BF16)
        out.append(p)
        x = x - p.astype(F32)
    out.append(x.astype(BF16))
    return out


def _dot(a, b):
    return jnp.dot(a, b, preferred_element_type=F32)


def _dot_nt(a, b):
    return lax.dot_general(a, b, (((1,), (1,)), ((), ())), preferred_element_type=F32)


def _dot_tn(a, b):
    return lax.dot_general(a, b, (((0,), (0,)), ((), ())), preferred_element_type=F32)


def _ada_kernel(c_ref, w_ref, b_ref, o_ref):
    a = _silu(c_ref[...]).astype(BF16)
    o_ref[...] = _dot(a, w_ref[...].astype(BF16)) + b_ref[...]


def _ada(c, w, b, tn=ADA_COL_TILE):
    m, d = c.shape
    n = w.shape[1]
    return pl.pallas_call(
        _ada_kernel,
        grid=(n // tn,),
        in_specs=[pl.BlockSpec((m, d), lambda j: (0, 0)),
                  pl.BlockSpec((d, tn), lambda j: (0, j)),
                  pl.BlockSpec((1, tn), lambda j: (0, j))],
        out_specs=pl.BlockSpec((m, tn), lambda j: (0, j)),
        out_shape=jax.ShapeDtypeStruct((m, n), F32),
        compiler_params=_params("arbitrary"),
        name="ada_mod",
    )(c, w, b)


def _prenorm(x_ref, g_ref, sc_ref, sh_ref, h_scr):
    h = _rms(x_ref[...], g_ref[...]) * (1.0 + sc_ref[...]) + sh_ref[...]
    h_scr[...] = h.astype(BF16)


def _in_proj_attn_kernel(x_ref, g_ref, sc_ref, sh_ref, w_ref, qkv_ref, krow_ref, vrow_ref, h_scr,
                         *, heads, dh, q_scale):
    j = pl.program_id(1)

    def product():
        return _dot(h_scr[...], w_ref[...])

    @pl.when(j == 0)
    def _():
        _prenorm(x_ref, g_ref, sc_ref, sh_ref, h_scr)
        qkv_ref[...] = (product() * q_scale).astype(BF16)

    for tile, rows_ref in ((1, krow_ref), (2, vrow_ref)):
        @pl.when(j == tile)
        def _(rows_ref=rows_ref):
            res = product()
            qkv_ref[...] = res.astype(BF16)
            for h in range(heads):
                rows_ref[pl.ds(h, res.shape[0], stride=heads), :] = res[:, h * dh:(h + 1) * dh]


def _in_proj_gla_kernel(x_ref, g_ref, sc_ref, sh_ref, w_ref, wl_ref, gla_ref, ol_ref, h_scr):
    @pl.when(pl.program_id(1) == 0)
    def _():
        _prenorm(x_ref, g_ref, sc_ref, sh_ref, h_scr)
        ol_ref[...] = _dot(h_scr[...], wl_ref[...])

    gla_ref[...] = _dot(h_scr[...], w_ref[...])


def _in_proj(x, g, sc, sh, w, wl, tm, rows_per_group, heads, dh):
    m, d = x.shape
    tn = heads * dh
    r = sc.shape[1]
    mod_spec = pl.BlockSpec((None, r, d), lambda i, j: ((i * tm) // rows_per_group, 0, 0))
    common = [pl.BlockSpec((tm, d), lambda i, j: (i, 0)),
              pl.BlockSpec((1, d), lambda i, j: (0, 0)),
              mod_spec, mod_spec]
    attn_kernel = functools.partial(_in_proj_attn_kernel, heads=heads, dh=dh,
                                    q_scale=dh ** -0.5 * LOG2E)
    qkv, krow, vrow = pl.pallas_call(
        attn_kernel,
        grid=(m // tm, 3),
        in_specs=common + [pl.BlockSpec((d, tn), lambda i, j: (0, j))],
        out_specs=[pl.BlockSpec((None, tm, tn), lambda i, j: (j, i, 0)),
                   pl.BlockSpec((tm * heads, dh), lambda i, j: (i, 0)),
                   pl.BlockSpec((tm * heads, dh), lambda i, j: (i, 0))],
        out_shape=[jax.ShapeDtypeStruct((3, m, tn), BF16),
                   jax.ShapeDtypeStruct((m * heads, dh), F32),
                   jax.ShapeDtypeStruct((m * heads, dh), F32)],
        scratch_shapes=[pltpu.VMEM((tm, d), BF16)],
        compiler_params=_params("arbitrary", "arbitrary"),
        name="in_proj_attn",
    )(x, g, sc, sh, w)
    gla, glow = pl.pallas_call(
        _in_proj_gla_kernel,
        grid=(m // tm, 3),
        in_specs=common + [pl.BlockSpec((d, tn), lambda i, j: (0, 3 + j)),
                           pl.BlockSpec((d, LANES), lambda i, j: (0, 0))],
        out_specs=[pl.BlockSpec((tm, tn), lambda i, j: (i, j)),
                   pl.BlockSpec((tm, LANES), lambda i, j: (i, 0))],
        out_shape=[jax.ShapeDtypeStruct((m, 3 * tn), F32),
                   jax.ShapeDtypeStruct((m, LANES), F32)],
        scratch_shapes=[pltpu.VMEM((tm, d), BF16)],
        compiler_params=_params("arbitrary", "arbitrary"),
        name="in_proj_gla",
    )(x, g, sc, sh, w, wl)
    return qkv, krow, vrow, gla, glow


def _sb_group(z, carry, uo, mask):
    n = z.shape[1] // KEY_BLOCK
    lb = _log2_sigmoid(z)
    lk = lb - z
    if mask is not None:
        lk = jnp.where(mask, lk, 0.0)
    hi, lo = _split_bf16(lk, 2)
    tails = [None] * n
    for s in reversed(range(n)):
        sl = slice(s * KEY_BLOCK, (s + 1) * KEY_BLOCK)
        ts = _dot(jnp.concatenate([hi[:, sl], lo[:, sl]], axis=1), uo)
        tails[s] = ts[:, :KEY_BLOCK] + carry
        carry = carry + ts[:, KEY_BLOCK:]
    a = jnp.exp2(lb + jnp.concatenate(tails, axis=1))
    if mask is not None:
        a = jnp.where(mask, a, 0.0)
    return a.astype(BF16), carry


def _sb_decode_step(q_ref, bias_ref, uo_ref, gather_ref, spread_ref, k_refs, v_refs, carry_scr,
                    acc_scr, heads):
    pages = len(k_refs)
    q = q_ref[...]
    rows = k_refs[0].shape[0]
    own = (lax.broadcasted_iota(jnp.int32, (pages * heads, rows), 1) % heads
           == lax.broadcasted_iota(jnp.int32, (pages * heads, rows), 0) % heads)
    z_all = jnp.concatenate([_dot_nt(q, r[...].astype(BF16)) for r in k_refs], axis=0)
    pieces = _split_bf16(jnp.where(own, z_all, 0.0), 2)
    zr = _dot(jnp.concatenate(pieces, axis=1), gather_ref[...])
    zr = zr + jnp.concatenate([bias_ref[...] * LOG2E] * pages, axis=0)
    lb = _log2_sigmoid(zr)
    hi, lo = _split_bf16(lb - zr, 2)
    ts = _dot(jnp.concatenate([hi, lo], axis=1), uo_ref[...])
    carry = carry_scr[...]
    tails = []
    for p in range(pages):
        tsp = ts[p * heads:(p + 1) * heads]
        tails.append(tsp[:, :KEY_BLOCK] + carry)
        carry = carry + tsp[:, KEY_BLOCK:]
    carry_scr[...] = carry
    a = jnp.exp2(lb + jnp.concatenate(tails, axis=0)).astype(BF16)
    a_rows = jnp.where(own, _dot(a, spread_ref[...]), 0.0)
    acc = acc_scr[...]
    for p in range(pages):
        acc = acc + _dot(a_rows[p * heads:(p + 1) * heads].astype(BF16), v_refs[p][...].astype(BF16))
    acc_scr[...] = acc


def _sb_attn_kernel(pt_ref, q_ref, k_ref, v_ref, bias_ref, uo_ref, qs_ref, bias_rows_ref,
                    gather_ref, spread_ref, *refs, tq, group, pages, heads):
    k_refs, cache_v_ref = refs[:pages], refs[pages]
    (o_ref, os_ref, carry_scr, acc_scr, dcarry_scr, dacc_scr, v_ring,
     v_sems) = refs[pages + 1:]
    qi = pl.program_id(2)
    h = pl.program_id(1)
    nq = pl.num_programs(2)
    n_pages = pt_ref.shape[1]

    step = (pl.program_id(0) * pl.num_programs(1) + h) * nq + qi
    n_steps = pl.num_programs(0) * pl.num_programs(1) * nq

    def v_copies(s):
        slot = lax.rem(s, V_RING_SLOTS)
        seq, grp = lax.div(s, nq), lax.rem(s, nq)
        return [pltpu.make_async_copy(
            cache_v_ref.at[pt_ref[seq, n_pages - 1 - (grp * pages + j)]],
            v_ring.at[slot, j], v_sems.at[slot]) for j in range(pages)]

    @pl.when(step == 0)
    def _():
        for s in range(V_RING_SLOTS - 1):
            for c in v_copies(jnp.int32(s)):
                c.start()

    @pl.when(step + V_RING_SLOTS - 1 < n_steps)
    def _():
        for c in v_copies(step + V_RING_SLOTS - 1):
            c.start(priority=1)

    for c in v_copies(step):
        c.wait()
    v_slot = lax.rem(step, V_RING_SLOTS)
    v_refs = [v_ring.at[v_slot, j] for j in range(pages)]
    q = q_ref[...]
    bias = bias_ref[h] * LOG2E
    uo = uo_ref[...]
    n_diag = tq // KEY_BLOCK

    @pl.when(qi == 0)
    def _():
        dcarry_scr[...] = jnp.zeros_like(dcarry_scr)
        dacc_scr[...] = jnp.zeros_like(dacc_scr)

    def visit(first_block, n, newest):
        rows = n * KEY_BLOCK
        start = pl.multiple_of(first_block * KEY_BLOCK, KEY_BLOCK)
        kb = k_ref[pl.ds(start, rows), :]
        vb = v_ref[pl.ds(start, rows), :]
        mask = None
        if newest:
            mask = (lax.broadcasted_iota(jnp.int32, (tq, rows), 1) + (start - qi * tq)
                    < lax.broadcasted_iota(jnp.int32, (tq, rows), 0))
            _sb_decode_step(qs_ref, bias_rows_ref, uo_ref, gather_ref, spread_ref, k_refs, v_refs,
                            dcarry_scr, dacc_scr, heads)
        a, carry = _sb_group(_dot_nt(q, kb) + bias, carry_scr[...], uo, mask)
        carry_scr[...] = carry
        acc_scr[...] += _dot(a, vb)

    carry_scr[...] = jnp.zeros_like(carry_scr)
    acc_scr[...] = jnp.zeros_like(acc_scr)
    n_tot = (qi + 1) * n_diag
    n_groups = n_tot // group
    leftover = lax.rem(n_tot, group)

    @pl.when(n_groups > 0)
    def _():
        visit(n_tot - group, group, True)

    def body(g, _):
        visit(n_tot - (g + 1) * group, group, False)
        return 0

    lax.fori_loop(1, n_groups, body, 0)
    size = group // 2
    while size >= n_diag:
        take = lax.rem(lax.div(leftover, size), 2) == 1
        is_first = jnp.logical_and(n_groups == 0, lax.div(leftover, 2 * size) == 0)
        for newest in (True, False):
            pl.when(jnp.logical_and(take, is_first == newest))(
                functools.partial(visit, lax.rem(leftover, size), size, newest))
        size //= 2
    o_ref[...] = acc_scr[...]

    @pl.when(qi == pl.num_programs(2) - 1)
    def _():
        os_ref[...] = dacc_scr[...]


def _sb_attention(qkv, b_sb, uo, q_s, cache_k, cache_v, page_table, bias_rows, batch, seq, heads,
                  dh, tq, group=ATTN_KEY_GROUP):
    m = qkv.shape[1]
    nq = seq // tq
    nb, n_pages = page_table.shape
    rows = cache_k.shape[1]
    assert nb == batch * heads and n_pages % nq == 0 and rows == KEY_BLOCK * heads
    pages = n_pages // nq
    kernel = functools.partial(_sb_attn_kernel, tq=tq, group=group, pages=pages, heads=heads)
    key_of_row = lax.broadcasted_iota(jnp.int32, (rows, KEY_BLOCK), 0) // heads
    gather1 = jnp.where(key_of_row == lax.broadcasted_iota(jnp.int32, (rows, KEY_BLOCK), 1), 1.0, 0.0)
    gather = jnp.concatenate([gather1] * 2, axis=0).astype(BF16)
    spread = gather1.T.astype(BF16)

    def page_spec(j):
        return pl.BlockSpec(
            (None, rows, dh),
            lambda b, h, i, pt, j=j: (pt[b * heads + h, n_pages - 1 - (i * pages + j)], 0, 0))

    const = lambda a: pl.BlockSpec(a.shape, lambda b, h, i, pt: (0,) * a.ndim)
    grid_spec = pltpu.PrefetchScalarGridSpec(
        num_scalar_prefetch=1,
        grid=(batch, heads, nq),
        in_specs=[pl.BlockSpec((None, tq, dh), lambda b, h, i, pt: (0, b * nq + i, h)),
                  pl.BlockSpec((None, seq, dh), lambda b, h, i, pt: (1, b, h)),
                  pl.BlockSpec((None, seq, dh), lambda b, h, i, pt: (2, b, h)),
                  pl.BlockSpec(memory_space=pltpu.SMEM),
                  const(uo),
                  pl.BlockSpec((None, heads, dh), lambda b, h, i, pt: (b * heads + h, 0, 0)),
                  const(bias_rows), const(gather), const(spread)]
                 + [page_spec(j) for j in range(pages)]
                 + [pl.BlockSpec(memory_space=pl.ANY)],
        out_specs=[pl.BlockSpec((tq, dh), lambda b, h, i, pt: (b * nq + i, h)),
                   pl.BlockSpec((None, heads, dh), lambda b, h, i, pt: (b * heads + h, 0, 0))],
        scratch_shapes=[pltpu.VMEM((tq, KEY_BLOCK), F32), pltpu.VMEM((tq, dh), F32),
                        pltpu.VMEM((heads, KEY_BLOCK), F32), pltpu.VMEM((heads, dh), F32),
                        pltpu.VMEM((V_RING_SLOTS, pages, rows, dh), F32),
                        pltpu.SemaphoreType.DMA((V_RING_SLOTS,))],
    )
    return pl.pallas_call(
        kernel,
        grid_spec=grid_spec,
        out_shape=[jax.ShapeDtypeStruct((m, heads * dh), F32),
                   jax.ShapeDtypeStruct((nb, heads, dh), F32)],
        compiler_params=_params("arbitrary", "arbitrary", "arbitrary"),
        name="sb_attention",
    )(page_table, qkv, qkv, qkv, b_sb, uo, q_s, bias_rows, gather, spread,
      *([cache_k] * pages), cache_v)


def _gla_prefix_matrix():
    t = lax.broadcasted_iota(jnp.int32, (GLA_CHUNK, GLA_CHUNK), 0)
    s = lax.broadcasted_iota(jnp.int32, (GLA_CHUNK, GLA_CHUNK), 1)
    return jnp.where(s <= t, 1.0, 0.0).astype(BF16)


def _gla_intra(qs, ks, bs, ones_bf16):
    n = len(qs)
    c = GLA_CHUNK
    t_i = lax.broadcasted_iota(jnp.int32, (c, c), 0)
    s_i = lax.broadcasted_iota(jnp.int32, (c, c), 1)
    atts = [jnp.zeros((c, c), F32)] * n
    h = c // 2
    while h >= GLA_SUB:
        th, sh = t_i // h, s_i // h
        pair = ((th - sh - 1) | ((th & 1) ^ 1)) == 0
        for ci in range(n):
            q, k, b = qs[ci], ks[ci], bs[ci]
            ref = jnp.concatenate(
                [jnp.broadcast_to(b[j + h - 1:j + h], (2 * h, b.shape[1])) for j in range(0, c, 2 * h)],
                axis=0)
            ql = (q * jnp.exp(jnp.minimum(b - ref, 0.0))).astype(BF16)
            kl = (k * jnp.exp(jnp.minimum(ref - b, 0.0))).astype(BF16)
            atts[ci] = atts[ci] + jnp.where(pair, _dot_nt(ql, kl), 0.0)
        h //= 2
    prods = []
    for ci in range(n):
        q, k, b = qs[ci], ks[ci], bs[ci]
        for lo_ in range(0, c, GLA_SUB):
            qi, ki, bi = (a[lo_:lo_ + GLA_SUB] for a in (q, k, b))
            prods += [qi * ki[s:s + 1] * jnp.exp(jnp.minimum(bi - bi[s:s + 1], 0.0))
                      for s in range(GLA_SUB)]
    rsum = _dot(jnp.concatenate(prods, axis=0).astype(BF16), ones_bf16)
    lane = lax.broadcasted_iota(jnp.int32, (GLA_SUB, LANES), 1)
    trow = lax.broadcasted_iota(jnp.int32, (GLA_SUB, LANES), 0)
    for ci in range(n):
        strips = []
        for i, lo_ in enumerate(range(0, c, GLA_SUB)):
            base = (ci * (c // GLA_SUB) + i) * GLA_SUB * GLA_SUB
            strip = jnp.zeros((GLA_SUB, LANES), F32)
            for s in range(GLA_SUB):
                strip = jnp.where(lane == lo_ + s,
                                  rsum[base + s * GLA_SUB:base + (s + 1) * GLA_SUB], strip)
            strips.append(jnp.where(lane - lo_ <= trow, strip, 0.0)[:, :c])
        atts[ci] = atts[ci] + jnp.concatenate(strips, axis=0)
    return atts


def _gla_prompt_kernel(q_ref, k_ref, v_ref, gl_ref, wg_ref, bg_ref, ones_ref, ll_ref, o_ref,
                       st_ref, st_scr, *, n_inner, scale):
    @pl.when(pl.program_id(2) == 0)
    def _():
        st_scr[...] = jnp.zeros_like(st_scr)

    c = GLA_CHUNK
    ll = ll_ref[...]
    g = _log_sigmoid(_dot(gl_ref[...].astype(BF16), wg_ref[...]) + bg_ref[...]) * (1.0 / GLA_GATE_NORM)
    chunks = [slice(ci * c, (ci + 1) * c) for ci in range(n_inner)]
    bs = [sum(_dot(ll, piece) for piece in _split_bf16(g[sl], 3)) for sl in chunks]
    qs = [q_ref[sl, :] * scale for sl in chunks]
    ks = [k_ref[sl, :] for sl in chunks]
    atts = _gla_intra(qs, ks, bs, ones_ref[...])
    st = st_scr[...]
    for ci, sl in enumerate(chunks):
        q, k, b = qs[ci], ks[ci], bs[ci]
        v16 = v_ref[sl, :].astype(BF16)
        o_ref[sl, :] = (_dot_nt((q * jnp.exp(b)).astype(BF16), st.astype(BF16))
                        + _dot(atts[ci].astype(BF16), v16))
        b_last = b[c - 1:c]
        kdec = (k * jnp.exp(b_last - b)).astype(BF16)
        st = st * jnp.exp(b_last) + _dot_tn(v16, kdec)
    st_scr[...] = st
    st_ref[...] = st


def _gla_prompt(proj, glow, wgu, b_gate, ones, batch, seq, heads, dk, dv, q_col, k_col, v_col,
                n_inner=GLA_CHUNKS_PER_STEP):
    m = proj.shape[0]
    n_inner = min(n_inner, seq // GLA_CHUNK)
    cb = n_inner * GLA_CHUNK
    nc = seq // cb
    kernel = functools.partial(_gla_prompt_kernel, n_inner=n_inner, scale=dk ** -0.5)
    return pl.pallas_call(
        kernel,
        grid=(batch, heads, nc),
        in_specs=[pl.BlockSpec((cb, dk), lambda b, h, c: (b * nc + c, q_col // dk + h)),
                  pl.BlockSpec((cb, dk), lambda b, h, c: (b * nc + c, k_col // dk + h)),
                  pl.BlockSpec((cb, dv), lambda b, h, c: (b * nc + c, v_col // dv + h)),
                  pl.BlockSpec((cb, LANES), lambda b, h, c: (b * nc + c, 0)),
                  pl.BlockSpec((LANES, dk), lambda b, h, c: (0, h)),
                  pl.BlockSpec((1, dk), lambda b, h, c: (0, h)),
                  pl.BlockSpec((LANES, LANES), lambda b, h, c: (0, 0)),
                  pl.BlockSpec((GLA_CHUNK, GLA_CHUNK), lambda b, h, c: (0, 0))],
        out_specs=[pl.BlockSpec((cb, dv), lambda b, h, c: (b * nc + c, h)),
                   pl.BlockSpec((None, None, dv, dk), lambda b, h, c: (b, h, 0, 0))],
        out_shape=[jax.ShapeDtypeStruct((m, heads * dv), F32),
                   jax.ShapeDtypeStruct((batch, heads, dv, dk), F32)],
        scratch_shapes=[pltpu.VMEM((dv, dk), F32)],
        compiler_params=_params("arbitrary", "arbitrary", "arbitrary"),
        name="gla_prompt",
    )(proj, proj, proj, glow, wgu, b_gate, ones, _gla_prefix_matrix())


def _gla_step_kernel(q_ref, k_ref, v_ref, gl_ref, wg_ref, bg_ref, s_ref, o_ref, so_ref, *,
                     heads, dk, dv, scale):
    eye = (lax.broadcasted_iota(jnp.int32, (dk, dk), 0)
           == lax.broadcasted_iota(jnp.int32, (dk, dk), 1))

    def column(x_row):
        return jnp.sum(jnp.where(eye, x_row, 0.0), axis=1, keepdims=True)

    for i in range(q_ref.shape[0]):
        g = _log_sigmoid(_dot(gl_ref[i].astype(BF16), wg_ref[...]) + bg_ref[...]) * (1.0 / GLA_GATE_NORM)
        q = q_ref[i] * scale
        k = k_ref[i]
        v = v_ref[i]
        for h in range(heads):
            ksl = slice(h * dk, (h + 1) * dk)
            vsl = slice(h * dv, (h + 1) * dv)
            s_new = column(jnp.exp(g[:, ksl])) * s_ref[i, h] + column(k[:, ksl]) * v[:, vsl]
            so_ref[i, h] = s_new
            o_ref[i, :, vsl] = jnp.sum(column(q[:, ksl]) * s_new, axis=0, keepdims=True)


def _gla_step(proj, glow, wgu, b_gate, state, dk, dv, q_col, k_col, v_col, seqs=GLA_STEP_SEQS):
    nb, heads = state.shape[0], state.shape[1]
    kw, vw = heads * dk, heads * dv
    kernel = functools.partial(_gla_step_kernel, heads=heads, dk=dk, dv=dv, scale=dk ** -0.5)
    proj3 = proj.reshape(nb, 1, proj.shape[1])
    glow3 = glow.reshape(nb, 1, LANES)
    o, s_new = pl.pallas_call(
        kernel,
        grid=(nb // seqs,),
        in_specs=[pl.BlockSpec((seqs, 1, kw), lambda b: (b, 0, q_col // kw)),
                  pl.BlockSpec((seqs, 1, kw), lambda b: (b, 0, k_col // kw)),
                  pl.BlockSpec((seqs, 1, vw), lambda b: (b, 0, v_col // vw)),
                  pl.BlockSpec((seqs, 1, LANES), lambda b: (b, 0, 0)),
                  pl.BlockSpec((LANES, kw), lambda b: (0, 0)),
                  pl.BlockSpec((1, kw), lambda b: (0, 0)),
                  pl.BlockSpec((seqs, heads, dk, dv), lambda b: (b, 0, 0, 0))],
        out_specs=[pl.BlockSpec((seqs, 1, vw), lambda b: (b, 0, 0)),
                   pl.BlockSpec((seqs, heads, dk, dv), lambda b: (b, 0, 0, 0))],
        out_shape=[jax.ShapeDtypeStruct((nb, 1, vw), F32),
                   jax.ShapeDtypeStruct(state.shape, F32)],
        compiler_params=_params("arbitrary"),
        name="gla_step",
    )(proj3, proj3, proj3, glow3, wgu, b_gate, state)
    return o.reshape(nb, vw), s_new


def _out_proj_kernel(osb_ref, og_ref, rg_ref, x_ref, gsb_ref, ggla_ref, w_ref, gpost_ref,
                     gt_ref, gpre_ref, sc_ref, sh_ref, x1_ref, h2_ref, mix_scr, *, dh, dv):
    sbw = osb_ref.shape[1]
    for h in range(sbw // dh):
        sl = slice(h * dh, (h + 1) * dh)
        mix_scr[:, sl] = _rms(osb_ref[:, sl], gsb_ref[...]).astype(BF16)
    for h in range(og_ref.shape[1] // dv):
        sl = slice(h * dv, (h + 1) * dv)
        y = _rms(og_ref[:, sl], ggla_ref[...]) * _silu(rg_ref[:, sl])
        mix_scr[:, sbw + h * dv:sbw + (h + 1) * dv] = y.astype(BF16)
    m = _dot(mix_scr[...], w_ref[...])
    x1 = x_ref[...] + gt_ref[...] * _rms(m, gpost_ref[...])
    x1_ref[...] = x1
    h2 = _rms(x1, gpre_ref[...]) * (1.0 + sc_ref[...]) + sh_ref[...]
    h2_ref[...] = h2.astype(BF16)


def _out_proj(osb, og, proj, rg_col, x, g_sb, g_gla, w, g_post, gt, g_pre, sc, sh, tm,
              rows_per_group):
    m, d = x.shape
    sbw, gw = osb.shape[1], og.shape[1]
    dh, dv = g_sb.shape[1], g_gla.shape[1]
    r = gt.shape[1]
    mod_spec = pl.BlockSpec((None, r, d), lambda i: ((i * tm) // rows_per_group, 0, 0))
    row = lambda width: pl.BlockSpec((1, width), lambda i: (0, 0))
    kernel = functools.partial(_out_proj_kernel, dh=dh, dv=dv)
    return pl.pallas_call(
        kernel,
        grid=(m // tm,),
        in_specs=[pl.BlockSpec((tm, sbw), lambda i: (i, 0)),
                  pl.BlockSpec((tm, gw), lambda i: (i, 0)),
                  pl.BlockSpec((tm, gw), lambda i: (i, rg_col // gw)),
                  pl.BlockSpec((tm, d), lambda i: (i, 0)),
                  row(dh), row(dv),
                  pl.BlockSpec((sbw + gw, d), lambda i: (0, 0)),
                  row(d), mod_spec, row(d), mod_spec, mod_spec],
        out_specs=[pl.BlockSpec((tm, d), lambda i: (i, 0)),
                   pl.BlockSpec((tm, d), lambda i: (i, 0))],
        out_shape=[jax.ShapeDtypeStruct((m, d), F32),
                   jax.ShapeDtypeStruct((m, d), BF16)],
        scratch_shapes=[pltpu.VMEM((tm, sbw + gw), BF16)],
        compiler_params=_params("arbitrary"),
        name="out_proj",
    )(osb, og, proj, x, g_sb, g_gla, w, g_post, gt, g_pre, sc, sh)


def _ffn_kernel(h_ref, wg_ref, wu_ref, wd_ref, x1_ref, gpost_ref, gt_ref, y_ref, acc_scr):
    j = pl.program_id(1)
    last = pl.num_programs(1) - 1

    def part():
        h = h_ref[...]
        t = _silu(_dot(h, wg_ref[...])) * _dot(h, wu_ref[...])
        return _dot(t.astype(BF16), wd_ref[...])

    @pl.when(j == 0)
    def _():
        acc_scr[...] = part()

    @pl.when(jnp.logical_and(j > 0, j < last))
    def _():
        acc_scr[...] += part()

    @pl.when(j == last)
    def _():
        y_ref[...] = x1_ref[...] + gt_ref[...] * _rms(acc_scr[...] + part(), gpost_ref[...])


def _ffn(h2, wg, wu, wd, x1, g_post, gt, tm, rows_per_group, tf=FFN_COL_TILE):
    m, d = x1.shape
    f = wg.shape[1]
    r = gt.shape[1]
    assert f // tf >= 2
    return pl.pallas_call(
        _ffn_kernel,
        grid=(m // tm, f // tf),
        in_specs=[pl.BlockSpec((tm, d), lambda i, j: (i, 0)),
                  pl.BlockSpec((d, tf), lambda i, j: (0, j)),
                  pl.BlockSpec((d, tf), lambda i, j: (0, j)),
                  pl.BlockSpec((tf, d), lambda i, j: (j, 0)),
                  pl.BlockSpec((tm, d), lambda i, j: (i, 0)),
                  pl.BlockSpec((1, d), lambda i, j: (0, 0)),
                  pl.BlockSpec((None, r, d), lambda i, j: ((i * tm) // rows_per_group, 0, 0))],
        out_specs=pl.BlockSpec((tm, d), lambda i, j: (i, 0)),
        out_shape=jax.ShapeDtypeStruct((m, d), F32),
        scratch_shapes=[pltpu.VMEM((tm, d), F32)],
        compiler_params=_params("arbitrary", "arbitrary"),
        name="ffn",
    )(h2, wg, wu, wd, x1, g_post, gt)


def _suffix_sum_matrix():
    j = lax.broadcasted_iota(jnp.int32, (2 * KEY_BLOCK, 2 * KEY_BLOCK), 0) % KEY_BLOCK
    s = lax.broadcasted_iota(jnp.int32, (2 * KEY_BLOCK, 2 * KEY_BLOCK), 1)
    return jnp.where((j > s) | (s >= KEY_BLOCK), 1.0, 0.0).astype(BF16)


def kernel(x_prompt, x_sample, c_prompt, c_sample, cache_k, cache_v, page_table, state_gla, w_ada, b_ada, g_pre_mix, w_in, b_sb, w_gate_up, b_gate, g_sb_out, g_gla_out, w_out, g_post_mix, g_pre_ffn, w_ffn_gate, w_ffn_up, w_ffn_down, g_post_ffn):
    batch, seq, d = x_prompt.shape
    nb = x_sample.shape[0]
    depth, n_pool, page, sb_heads, dh = cache_k.shape
    gla_heads, dk, dv = state_gla.shape[2:]
    rank = w_gate_up.shape[1]
    assert depth == 1 and x_sample.shape[1] == 1
    sbw, kw, vw = sb_heads * dh, gla_heads * dk, gla_heads * dv
    assert 2 * kw == sbw and vw == sbw
    q_col, k_col, v_col, rg_col = 0, kw, sbw, 2 * sbw
    main = 3 * sbw + 2 * kw + 2 * vw

    row = lambda a: a.reshape(1, -1)
    uo = _suffix_sum_matrix()
    ones = jnp.ones((LANES, LANES), BF16)

    w_in0 = w_in[0]
    w_main = w_in0.astype(BF16)
    w_low = jnp.pad(w_in0[:, main:], ((0, 0), (0, LANES - rank))).astype(BF16)
    wgu = jnp.pad(w_gate_up[0], ((0, LANES - rank), (0, 0))).astype(BF16)
    w_o = w_out[0].astype(BF16)
    w_fg, w_fu, w_fd = (w[0].astype(BF16) for w in (w_ffn_gate, w_ffn_up, w_ffn_down))

    n_c = batch + nb
    c_all = jnp.pad(jnp.concatenate([c_prompt, c_sample], axis=0), ((0, -n_c % SUBLANES), (0, 0)))
    mod = _ada(c_all, w_ada[0], row(b_ada[0]))
    mod_p = [a.reshape(batch, 1, d) for a in jnp.split(mod[:batch], 6, axis=-1)]
    mod_s = [a.reshape(1, nb, d) for a in jnp.split(mod[batch:n_c], 6, axis=-1)]

    gains = dict(g_sb=row(g_sb_out[0]), g_gla=row(g_gla_out[0]), g_post=row(g_post_mix[0]),
                 g_pre=row(g_pre_ffn[0]))

    def rest_of_layer(x2, osb, og, proj, mods, tm, rows_per_group, ffn_tm):
        sh1, sc1, gt1, sh2, sc2, gt2 = mods
        x1, h2 = _out_proj(osb, og, proj, rg_col, x2, gains["g_sb"], gains["g_gla"], w_o,
                           gains["g_post"], gt1, gains["g_pre"], sc2, sh2, tm, rows_per_group)
        return _ffn(h2, w_fg, w_fu, w_fd, x1, row(g_post_ffn[0]), gt2, ffn_tm, rows_per_group)

    xp = x_prompt.reshape(batch * seq, d)
    xs = x_sample.reshape(nb, d)
    qkv_p, k_p, v_p, gla_p, glow_p = _in_proj(xp, row(g_pre_mix[0]), mod_p[1], mod_p[0], w_main, w_low,
                                              IN_PROJ_ROW_TILE, seq, sb_heads, dh)
    qkv_s, k_s, v_s, gla_s, glow_s = _in_proj(xs, row(g_pre_mix[0]), mod_s[1], mod_s[0], w_main, w_low,
                                              nb, nb, sb_heads, dh)

    bias_rows = jnp.broadcast_to(b_sb[0][:, None], (sb_heads, KEY_BLOCK))
    osb_p, osb_s = _sb_attention(qkv_p, b_sb[0], uo, qkv_s[0].reshape(nb, sb_heads, dh),
                                 cache_k[0].reshape(n_pool, page * sb_heads, dh),
                                 cache_v[0].reshape(n_pool, page * sb_heads, dh),
                                 page_table, bias_rows, batch, seq, sb_heads, dh,
                                 tq=ATTN_QUERY_TILE)

    og_p, st_p = _gla_prompt(gla_p, glow_p, wgu, row(b_gate[0]), ones, batch, seq, gla_heads,
                             dk, dv, q_col, k_col, v_col)
    y_p = rest_of_layer(xp, osb_p, og_p, gla_p, mod_p, OUT_PROJ_ROW_TILE, seq, FFN_ROW_TILE)
    og_s, st_s = _gla_step(gla_s, glow_s, wgu, row(b_gate[0]), state_gla[0], dk, dv,
                           q_col, k_col, v_col)
    y_s = rest_of_layer(xs, osb_s.reshape(nb, sbw), og_s, gla_s, mod_s, nb, nb, nb)

    kv = lambda a, n: a.reshape(1, n, -1, sb_heads, dh)
    return (y_p.reshape(batch, seq, d), y_s.reshape(nb, 1, d),
            kv(k_p, batch), kv(v_p, batch),
            jnp.swapaxes(st_p, -1, -2)[None],
            kv(k_s, nb), kv(v_s, nb), st_s[None])
```
